```python
import math
import jax
import jax.numpy as jnp
from jax import lax
import numpy as np

D_MODEL = 2048
BATCH = 4
SEQ = 2048
DEPTH = 2

M_HEADS = 4
M_DH = 256
M_W = M_HEADS * M_DH
CONV_K = 4
A_HEADS = 8
A_DH = 64
A_W = A_HEADS * 2 * A_DH
ROT_DIM = A_DH // 4
ROPE_THETA = 500000.0
Q_BLOCK = 128
G_HEADS = 8
G_DK = 128
G_DV = 128
G_WK = G_HEADS * G_DK
G_WV = G_HEADS * G_DV
CHUNK = 64
N_BRANCH = 3
D_FF = 4 * D_MODEL
ALPHA = (2 * DEPTH) ** 0.25
BETA = (8 * DEPTH) ** -0.25
LN_EPS = 1e-5
NORM_EPS = 1e-6
NEG_BIG = -1e30
F_FLOOR = 1e-30

IN_SIZES = (M_W, M_W, M_W, M_W, M_HEADS, M_HEADS,
            A_W, A_W, A_W,
            G_WK, G_WK, G_WV, G_WV,
            N_BRANCH * D_MODEL)
IN_WIDTH = sum(IN_SIZES)

kernel_name = 'hybrid_mlstm_diffattn_hgrn2_deepnorm'


def layer_norm(x, w, b):
    xf = x.astype(jnp.float32)
    mu = jnp.mean(xf, -1, keepdims=True)
    var = jnp.mean(jnp.square(xf - mu), -1, keepdims=True)
    return ((xf - mu) * lax.rsqrt(var + LN_EPS)).astype(x.dtype) * w + b


def head_rms_norm(x, w):
    xf = x.astype(jnp.float32)
    H, d = x.shape[-2:]
    y = xf * lax.rsqrt(jnp.mean(jnp.square(xf), -1, keepdims=True) + NORM_EPS)
    return y * w.astype(jnp.float32).reshape(H, d)


def causal_dwconv(x, w, b):
    C = x.shape[-1]
    y = lax.conv_general_dilated(x, w[:, None, :].astype(x.dtype), window_strides=(1,),
                                 padding=[(CONV_K - 1, 0)],
                                 dimension_numbers=('NWC', 'WIO', 'NWC'),
                                 feature_group_count=C)
    return y + b


def rope_cos_sin(positions):
    inv = jnp.power(jnp.float32(ROPE_THETA), -jnp.arange(0, ROT_DIM, 2, dtype=jnp.float32) / ROT_DIM)
    ang = positions.astype(jnp.float32)[..., None] * inv
    return jnp.cos(ang), jnp.sin(ang)


def partial_rope(x, cos, sin):
    half = ROT_DIM // 2
    c = cos[:, :, None, None, :].astype(x.dtype)
    s = sin[:, :, None, None, :].astype(x.dtype)
    x1, x2, rest = x[..., :half], x[..., half:ROT_DIM], x[..., ROT_DIM:]
    return jnp.concatenate([x1 * c - x2 * s, x2 * c + x1 * s, rest], axis=-1)


def to_chunks(t):
    B, S, H = t.shape[:3]
    t = t.reshape((B, S // CHUNK, CHUNK, H) + t.shape[3:])
    return jnp.moveaxis(t, (1, 3), (0, 2))


def from_chunks(t):
    n, B, H, C = t.shape[:4]
    t = jnp.moveaxis(t, (0, 2), (1, 3))
    return t.reshape((B, n * C, H) + t.shape[4:])


def mlstm_chunkwise(q, k, v, i_pre, f_pre):
    B, S, H, d = q.shape
    f32 = jnp.float32
    q = q.astype(f32)
    k = k.astype(f32) * (d ** -0.5)
    v = v.astype(f32)
    log_f = jax.nn.log_sigmoid(f_pre.astype(f32))
    log_i = i_pre.astype(f32)
    causal = jnp.tril(jnp.ones((CHUNK, CHUNK), dtype=bool))

    def step(carry, inp):
        C_prev, n_prev, m_prev = carry
        qc, kc, vc, ic, lfc = inp
        b = jnp.cumsum(lfc, axis=-1)
        log_d = b[..., :, None] - b[..., None, :] + ic[..., None, :]
        log_d = jnp.where(causal, log_d, NEG_BIG)
        log_inter = b + m_prev[..., None]
        m_t = jnp.maximum(log_inter, jnp.max(log_d, -1))
        w_inter = jnp.exp(log_inter - m_t)
        s = jnp.einsum('bhtd,bhsd->bhts', qc, kc) * jnp.exp(log_d - m_t[..., None])
        num = (jnp.einsum('bhts,bhse->bhte', s, vc)
               + w_inter[..., None] * jnp.einsum('bhtd,bhde->bhte', qc, C_prev))
        den = jnp.sum(s, -1) + w_inter * jnp.einsum('bhtd,bhd->bht', qc, n_prev)
        h = num / jnp.maximum(jnp.abs(den), jnp.exp(-m_t))[..., None]
        g = b[..., -1]
        log_w = g[..., None] - b + ic
        m_new = jnp.maximum(g + m_prev, jnp.max(log_w, -1))
        w_s = jnp.exp(log_w - m_new[..., None])
        decay = jnp.exp(g + m_prev - m_new)
        C_new = decay[..., None, None] * C_prev + jnp.einsum('bhs,bhsd,bhse->bhde', w_s, kc, vc)
        n_new = decay[..., None] * n_prev + jnp.einsum('bhs,bhsd->bhd', w_s, kc)
        return (C_new, n_new, m_new), h

    init = (jnp.zeros((B, H, d, d), f32), jnp.zeros((B, H, d), f32), jnp.zeros((B, H), f32))
    _, h = lax.scan(step, init, (to_chunks(q), to_chunks(k), to_chunks(v),
                                 to_chunks(log_i), to_chunks(log_f)))
    return from_chunks(h)


def hgrn2_chunkwise(q, k, v, log_f):
    B, S, H, dk = q.shape
    dv = v.shape[-1]
    f32 = jnp.float32
    q, k, v, log_f = (t.astype(f32) for t in (q, k, v, log_f))
    causal = jnp.tril(jnp.ones((CHUNK, CHUNK), dtype=bool))[:, :, None]

    def step(S_prev, inp):
        qc, kc, vc, lfc = inp
        b = jnp.cumsum(lfc, axis=2)
        rel = b[:, :, :, None, :] - b[:, :, None, :, :]
        rel = jnp.where(causal, rel, NEG_BIG)
        attn = jnp.einsum('bhtk,bhsk,bhtsk->bhts', qc, kc, jnp.exp(rel))
        o = (jnp.einsum('bhts,bhsv->bhtv', attn, vc)
             + jnp.einsum('bhtk,bhkv->bhtv', qc * jnp.exp(b), S_prev))
        b_last = b[:, :, -1]
        S_new = (jnp.exp(b_last)[..., None] * S_prev
                 + jnp.einsum('bhsk,bhsv->bhkv', kc * jnp.exp(b_last[:, :, None] - b), vc))
        return S_new, o

    _, o = lax.scan(step, jnp.zeros((B, H, dk, dv), f32),
                    (to_chunks(q), to_chunks(k), to_chunks(v), to_chunks(log_f)))
    return from_chunks(o)


def diff_attention(q, k, v, lam):
    B, S, H, _, d = q.shape
    nb = S // Q_BLOCK
    scale = d ** -0.5
    kpos = jnp.arange(S)
    qb = jnp.moveaxis(q.reshape(B, nb, Q_BLOCK, H, 2, d), 1, 0)

    def block(args):
        qblk, start = args
        s = jnp.einsum('bqhnd,bkhnd->bhnqk', qblk, k).astype(jnp.float32) * scale
        qpos = start + jnp.arange(Q_BLOCK)
        s = jnp.where(kpos[None, :] <= qpos[:, None], s, NEG_BIG)
        p = jax.nn.softmax(s, axis=-1)
        a = p[:, :, 0] - lam * p[:, :, 1]
        return jnp.einsum('bhqk,bkhe->bqhe', a.astype(v.dtype), v)

    out = lax.map(block, (qb, jnp.arange(nb) * Q_BLOCK))
    return jnp.moveaxis(out, 0, 1).reshape(B, S, H, 2 * d)


def mixer_sublayer(u, layer_idx, cos, sin, lb, w_in, conv_w, conv_b, gate_b, m_norm_w,
                   lam_vecs, a_norm_w, g_norm_w, p_m, p_a, p_g, w_out):
    B, S, _ = u.shape
    f32 = jnp.float32
    split_at = np.cumsum(IN_SIZES)[:-1].tolist()
    (mq, mk, mv, mo, mi, mf, aq, ak, av, gq, gf, gi, gg, gate_pre) = jnp.split(u @ w_in, split_at, axis=-1)

    qk = jax.nn.silu(causal_dwconv(jnp.concatenate([mq, mk], axis=-1), conv_w, conv_b))
    mq, mk = jnp.split(qk, 2, axis=-1)
    hshape = (B, S, M_HEADS, M_DH)
    hm = mlstm_chunkwise(mq.reshape(hshape), mk.reshape(hshape), mv.reshape(hshape),
                         mi + gate_b[:M_HEADS], mf + gate_b[M_HEADS:])
    hm = head_rms_norm(hm, m_norm_w) * jax.nn.sigmoid(mo.astype(f32)).reshape(hshape)
    hm = hm.astype(u.dtype).reshape(B, S, M_W)

    aq = partial_rope(aq.reshape(B, S, A_HEADS, 2, A_DH), cos, sin)
    ak = partial_rope(ak.reshape(B, S, A_HEADS, 2, A_DH), cos, sin)
    lam_init = 0.8 - 0.6 * math.exp(-0.3 * layer_idx)
    lv = lam_vecs.astype(f32)
    lam = jnp.exp(jnp.sum(lv[0] * lv[1])) - jnp.exp(jnp.sum(lv[2] * lv[3])) + lam_init
    ha = diff_attention(aq, ak, av.reshape(B, S, A_HEADS, 2 * A_DH), lam)
    ha = (head_rms_norm(ha, a_norm_w) * (1.0 - lam_init)).astype(u.dtype).reshape(B, S, A_W)

    f_g = lb + (1.0 - lb) * jax.nn.sigmoid(gf.astype(f32))
    log_f = jnp.log(jnp.maximum(f_g, F_FLOOR))
    k_g = 1.0 - f_g
    kshape = (B, S, G_HEADS, G_DK)
    hg = hgrn2_chunkwise(jax.nn.silu(gq).reshape(kshape), k_g.reshape(kshape),
                         gi.reshape(B, S, G_HEADS, G_DV), log_f.reshape(kshape))
    hg = head_rms_norm(hg, g_norm_w) * jax.nn.silu(gg.astype(f32)).reshape(B, S, G_HEADS, G_DV)
    hg = hg.astype(u.dtype).reshape(B, S, G_WV)

    gates = jax.nn.sigmoid(gate_pre).reshape(B, S, N_BRANCH, D_MODEL)
    y = (gates[:, :, 0] * (hm @ p_m) + gates[:, :, 1] * (ha @ p_a) + gates[:, :, 2] * (hg @ p_g))
    return y @ w_out


def squared_relu_mlp(u, w_up, w_down):
    return jnp.square(jax.nn.relu(u @ w_up)) @ w_down


def setup_inputs(seed: int = 0) -> dict:
    key = jax.random.key(seed)
    ks = iter(jax.random.split(key, 32))
    L = DEPTH

    def nrm(shape, scale):
        return scale * jax.random.normal(next(ks), shape, jnp.float32)

    x = nrm((BATCH, SEQ, D_MODEL), 1.0)
    positions = jnp.broadcast_to(jnp.arange(SEQ, dtype=jnp.int32), (BATCH, SEQ))
    ln0_w = 1.0 + nrm((D_MODEL,), 0.02)
    ln0_b = nrm((D_MODEL,), 0.02)
    w_in = nrm((L, D_MODEL, IN_WIDTH), D_MODEL ** -0.5)
    m_conv_w = nrm((L, CONV_K, 2 * M_W), CONV_K ** -0.5)
    m_conv_b = nrm((L, 2 * M_W), 0.02)
    f_bias = jnp.linspace(3.0, 6.0, M_HEADS, dtype=jnp.float32)
    m_gate_b = jnp.concatenate([nrm((L, M_HEADS), 0.1), f_bias + nrm((L, M_HEADS), 0.1)], axis=-1)
    m_norm_w = 1.0 + nrm((L, M_W), 0.02)
    a_lambda = nrm((L, 4, A_DH), 0.1)
    a_norm_w = 1.0 + nrm((L, A_W), 0.02)
    g_lb_logits = nrm((L, G_WK), 0.5)
    g_norm_w = 1.0 + nrm((L, G_WV), 0.02)
    p_m = nrm((L, M_W, D_MODEL), M_W ** -0.5)
    p_a = nrm((L, A_W, D_MODEL), A_W ** -0.5)
    p_g = nrm((L, G_WV, D_MODEL), G_WV ** -0.5)
    w_out = nrm((L, D_MODEL, D_MODEL), BETA * D_MODEL ** -0.5)
    ln1_w = 1.0 + nrm((L, D_MODEL), 0.02)
    ln1_b = nrm((L, D_MODEL), 0.02)
    w_up = nrm((L, D_MODEL, D_FF), D_MODEL ** -0.5)
    w_down = nrm((L, D_FF, D_MODEL), BETA * D_FF ** -0.5)
    ln2_w = 1.0 + nrm((L, D_MODEL), 0.02)
    ln2_b = nrm((L, D_MODEL), 0.02)
    return {'x': x, 'positions': positions, 'ln0_w': ln0_w, 'ln0_b': ln0_b, 'w_in': w_in,
            'm_conv_w': m_conv_w, 'm_conv_b': m_conv_b, 'm_gate_b': m_gate_b, 'm_norm_w': m_norm_w,
            'a_lambda': a_lambda, 'a_norm_w': a_norm_w, 'g_lb_logits': g_lb_logits,
            'g_norm_w': g_norm_w, 'p_m': p_m, 'p_a': p_a, 'p_g': p_g, 'w_out': w_out,
            'ln1_w': ln1_w, 'ln1_b': ln1_b, 'w_up': w_up, 'w_down': w_down,
            'ln2_w': ln2_w, 'ln2_b': ln2_b}


def reference(x, positions, ln0_w, ln0_b, w_in, m_conv_w, m_conv_b, m_gate_b, m_norm_w,
              a_lambda, a_norm_w, g_lb_logits, g_norm_w, p_m, p_a, p_g, w_out,
              ln1_w, ln1_b, w_up, w_down, ln2_w, ln2_b):
    cos, sin = rope_cos_sin(positions)
    p_lb = jax.nn.softmax(g_lb_logits.astype(jnp.float32), axis=0)
    lower_bounds = jnp.cumsum(p_lb, axis=0) - p_lb[0]
    h = layer_norm(x, ln0_w, ln0_b)
    for l in range(DEPTH):
        mix = mixer_sublayer(h, l, cos, sin, lower_bounds[l], w_in[l], m_conv_w[l], m_conv_b[l],
                             m_gate_b[l], m_norm_w[l], a_lambda[l], a_norm_w[l], g_norm_w[l],
                             p_m[l], p_a[l], p_g[l], w_out[l])
        h = layer_norm(ALPHA * h + mix, ln1_w[l], ln1_b[l])
        ff = squared_relu_mlp(h, w_up[l], w_down[l])
        h = layer_norm(ALPHA * h + ff, ln2_w[l], ln2_b[l])
    return h
```

```python
import functools
import math

import jax
import jax.numpy as jnp
from jax import lax
from jax.experimental import pallas as pl
from jax.experimental.pallas import tpu as pltpu

F32 = jnp.float32
BF16 = jnp.bfloat16

D_MODEL = 2048
DEPTH = 2
M_HEADS = 4
M_DH = 256
M_W = M_HEADS * M_DH
CONV_K = 4
A_HEADS = 8
A_DH = 64
A_W = A_HEADS * 2 * A_DH
ROT_DIM = A_DH // 4
ROPE_THETA = 500000.0
G_HEADS = 8
G_DK = 128
G_WK = G_HEADS * G_DK
G_WV = G_HEADS * G_DK
CHUNK = 64
SUB = 16
N_BRANCH = 3
D_FF = 4 * D_MODEL
ALPHA = (2 * DEPTH) ** 0.25
LN_EPS = 1e-5
NORM_EPS = 1e-6
NEG_BIG = -1e30
F_FLOOR = 1e-30

OFF_MQ, OFF_MK, OFF_MV, OFF_MO = 0, M_W, 2 * M_W, 3 * M_W
OFF_AQ = 4 * M_W
OFF_AK = OFF_AQ + A_W
OFF_AV = OFF_AK + A_W
OFF_GQ = OFF_AV + A_W
OFF_GF = OFF_GQ + G_WK
OFF_GI = OFF_GF + G_WK
OFF_GG = OFF_GI + G_WV
OFF_GATE = OFF_GG + G_WV
PROJ_W = OFF_GATE + N_BRANCH * D_MODEL
LANES = 128
VMEM_LIMIT = 48 * 1024 * 1024

NT_DIMS = (((1,), (1,)), ((), ()))
TN_DIMS = (((0,), (0,)), ((), ()))


def _params(*sem):
    return pltpu.CompilerParams(dimension_semantics=sem, vmem_limit_bytes=VMEM_LIMIT)


def _sigmoid(x):
    return 1.0 / (1.0 + jnp.exp(-x))


def _layer_norm(x, w, b):
    mu = jnp.mean(x, -1, keepdims=True)
    xc = x - mu
    var = jnp.mean(xc * xc, -1, keepdims=True)
    return xc * lax.rsqrt(var + LN_EPS) * w + b


def _rms_norm(x, w):
    return x * lax.rsqrt(jnp.mean(x * x, -1, keepdims=True) + NORM_EPS) * w


def _chunk_cumsum(x, chunk):
    row = lax.broadcasted_iota(jnp.int32, x.shape, 0) & (chunk - 1)
    sh = 1
    while sh < chunk:
        x = x + jnp.where(row >= sh, pltpu.roll(x, sh, axis=0), 0.0)
        sh *= 2
    return x


def _ln_kernel(x_ref, w_ref, b_ref, o_ref, obf_ref):
    y = _layer_norm(x_ref[...], w_ref[...], b_ref[...])
    o_ref[...] = y
    obf_ref[...] = y.astype(BF16)


def _ln(x, w, b, tm=256):
    m, d = x.shape
    return pl.pallas_call(
        _ln_kernel,
        out_shape=(jax.ShapeDtypeStruct((m, d), F32), jax.ShapeDtypeStruct((m, d), BF16)),
        grid=(m // tm,),
        in_specs=[pl.BlockSpec((tm, d), lambda i: (i, 0)),
                  pl.BlockSpec((1, d), lambda i: (0, 0)),
                  pl.BlockSpec((1, d), lambda i: (0, 0))],
        out_specs=(pl.BlockSpec((tm, d), lambda i: (i, 0)),
                   pl.BlockSpec((tm, d), lambda i: (i, 0))),
        compiler_params=_params("parallel"),
        name="ln0",
    )(x, w.reshape(1, d), b.reshape(1, d))


def _mm_kernel(a_ref, b_ref, o_ref):
    o_ref[...] = jnp.dot(a_ref[...], b_ref[...], preferred_element_type=F32).astype(o_ref.dtype)


def _matmul(a, b, out_dtype, tm, tn, name):
    m, k = a.shape
    n = b.shape[1]
    return pl.pallas_call(
        _mm_kernel,
        out_shape=jax.ShapeDtypeStruct((m, n), out_dtype),
        grid=(m // tm, n // tn),
        in_specs=[pl.BlockSpec((tm, k), lambda i, j: (i, 0)),
                  pl.BlockSpec((k, tn), lambda i, j: (0, j))],
        out_specs=pl.BlockSpec((tm, tn), lambda i, j: (i, j)),
        compiler_params=_params("parallel", "arbitrary"),
        name=name,
    )(a, b)


def _mlstm_kernel(q_ref, k_ref, v_ref, og_ref, g_ref, gb_ref, cwq_ref, cwk_ref, cbq_ref, cbk_ref,
                  nw_ref, out_ref, qpad, kpad, qs, ks, ps, c_s, n_s, m_s):
    blk = pl.program_id(2)
    tb = q_ref.shape[0]

    @pl.when(blk == 0)
    def _():
        qpad[0:8, :] = jnp.zeros((8, M_DH), F32)
        kpad[0:8, :] = jnp.zeros((8, M_DH), F32)
        c_s[...] = jnp.zeros_like(c_s)
        n_s[...] = jnp.zeros_like(n_s)
        m_s[...] = jnp.zeros_like(m_s)

    @pl.when(blk > 0)
    def _():
        qpad[0:8, :] = qpad[tb:tb + 8, :]
        kpad[0:8, :] = kpad[tb:tb + 8, :]

    qpad[8:8 + tb, :] = q_ref[...]
    kpad[8:8 + tb, :] = k_ref[...]

    def conv_silu(pad, w_ref, b_ref):
        y = b_ref[...]
        for j in range(CONV_K):
            off = 8 - (CONV_K - 1) + j
            y = y + w_ref[j:j + 1, :] * pad[off:off + tb, :]
        return y * _sigmoid(y)

    qs[...] = conv_silu(qpad, cwq_ref, cbq_ref)
    ks[...] = conv_silu(kpad, cwk_ref, cbk_ref) * (M_DH ** -0.5)

    gb = g_ref[...] + gb_ref[...]
    log_f = jnp.minimum(gb, 0.0) - jnp.log(1.0 + jnp.exp(-jnp.abs(gb)))
    bcum = _chunk_cumsum(log_f, CHUNK)
    lane = lax.broadcasted_iota(jnp.int32, gb.shape, 1)
    ps[...] = jnp.where(lane == 0, gb, bcum)

    ri = lax.broadcasted_iota(jnp.int32, (CHUNK, CHUNK), 0)
    ci = lax.broadcasted_iota(jnp.int32, (CHUNK, CHUNK), 1)
    causal = ri >= ci

    def chunk_step(c, carry):
        r0 = pl.multiple_of(c * CHUNK, CHUNK)
        pc = ps[pl.ds(r0, CHUNK), :]
        pt = pc.T
        i_col, b_col = pc[:, 0:1], pc[:, 1:2]
        i_row, b_row = pt[0:1, :], pt[1:2, :]
        qc = qs[pl.ds(r0, CHUNK), :]
        kc = ks[pl.ds(r0, CHUNK), :]
        vc = v_ref[pl.ds(r0, CHUNK), :].astype(BF16)
        qb = qc.astype(BF16)
        m_prev = m_s[...]
        c_prev = c_s[...]
        n_prev = n_s[...]

        log_d = jnp.where(causal, b_col - b_row + i_row, NEG_BIG)
        log_inter = b_col + m_prev
        m_t = jnp.maximum(log_inter, jnp.max(log_d, -1, keepdims=True))
        w_inter = jnp.exp(log_inter - m_t)
        s = lax.dot_general(qb, kc.astype(BF16), NT_DIMS, preferred_element_type=F32)
        s = s * jnp.exp(log_d - m_t)
        num = (jnp.dot(s.astype(BF16), vc, preferred_element_type=F32)
               + w_inter * jnp.dot(qb, c_prev.astype(BF16), preferred_element_type=F32))
        den = jnp.sum(s, -1, keepdims=True) + w_inter * jnp.sum(qc * n_prev, -1, keepdims=True)
        h = num / jnp.maximum(jnp.abs(den), jnp.exp(-m_t))

        g = b_col[CHUNK - 1:CHUNK, :]
        log_w = g - b_col + i_col
        m_new = jnp.maximum(g + m_prev, jnp.max(log_w, 0, keepdims=True))
        decay = jnp.exp(g + m_prev - m_new)
        kw = kc * jnp.exp(log_w - m_new)
        c_s[...] = decay * c_prev + lax.dot_general(kw.astype(BF16), vc, TN_DIMS,
                                                    preferred_element_type=F32)
        n_s[...] = decay * n_prev + jnp.sum(kw, 0, keepdims=True)
        m_s[...] = m_new

        hn = _rms_norm(h, nw_ref[...]) * _sigmoid(og_ref[pl.ds(r0, CHUNK), :])
        out_ref[pl.ds(r0, CHUNK), :] = hn.astype(out_ref.dtype)
        return carry

    lax.fori_loop(0, tb // CHUNK, chunk_step, 0)


def _mlstm(proj, gates, gate_b, conv_w, conv_b, norm_w, batch, seq, tb=512):
    nblk = seq // tb
    wq = M_DH
    row = lambda b, h, s: b * nblk + s
    col = lambda off: (lambda b, h, s: (row(b, h, s), off // wq + h))
    return pl.pallas_call(
        _mlstm_kernel,
        out_shape=jax.ShapeDtypeStruct((batch * seq, M_W), BF16),
        grid=(batch, M_HEADS, nblk),
        in_specs=[pl.BlockSpec((tb, wq), col(OFF_MQ)),
                  pl.BlockSpec((tb, wq), col(OFF_MK)),
                  pl.BlockSpec((tb, wq), col(OFF_MV)),
                  pl.BlockSpec((tb, wq), col(OFF_MO)),
                  pl.BlockSpec((tb, LANES), lambda b, h, s: (row(b, h, s), h)),
                  pl.BlockSpec((1, LANES), lambda b, h, s: (0, h)),
                  pl.BlockSpec((CONV_K, wq), lambda b, h, s: (0, h)),
                  pl.BlockSpec((CONV_K, wq), lambda b, h, s: (0, M_HEADS + h)),
                  pl.BlockSpec((1, wq), lambda b, h, s: (0, h)),
                  pl.BlockSpec((1, wq), lambda b, h, s: (0, M_HEADS + h)),
                  pl.BlockSpec((1, wq), lambda b, h, s: (0, h))],
        out_specs=pl.BlockSpec((tb, wq), lambda b, h, s: (row(b, h, s), h)),
        scratch_shapes=[pltpu.VMEM((tb + 8, wq), F32), pltpu.VMEM((tb + 8, wq), F32),
                        pltpu.VMEM((tb, wq), F32), pltpu.VMEM((tb, wq), F32),
                        pltpu.VMEM((tb, LANES), F32),
                        pltpu.VMEM((wq, wq), F32), pltpu.VMEM((1, wq), F32), pltpu.VMEM((1, 1), F32)],
        compiler_params=_params("parallel", "parallel", "arbitrary"),
        name="mlstm",
    )(proj, proj, proj, proj, gates, gate_b, conv_w, conv_w,
      conv_b.reshape(1, -1), conv_b.reshape(1, -1), norm_w.reshape(1, -1))


def _rope_kernel(pos_ref, inv_ref, q_ref, k_ref, v_ref, qo_ref, ko_ref, vo_ref):
    ang = pos_ref[...].astype(F32) * inv_ref[...]
    lane = lax.broadcasted_iota(jnp.int32, ang.shape, 1) & (A_DH - 1)
    half = ROT_DIM // 2
    cos, sin = jnp.cos(ang), jnp.sin(ang)
    c_same = jnp.where(lane < ROT_DIM, cos, 1.0)
    c_lo = jnp.where((lane >= half) & (lane < ROT_DIM), sin, 0.0)
    c_hi = jnp.where(lane < half, -sin, 0.0)
    for src, dst in ((q_ref, qo_ref), (k_ref, ko_ref)):
        for g in range(A_W // LANES):
            sl = slice(g * LANES, (g + 1) * LANES)
            x = src[:, sl]
            y = (x * c_same + pltpu.roll(x, half, axis=1) * c_lo
                 + pltpu.roll(x, LANES - half, axis=1) * c_hi)
            dst[:, sl] = y.astype(dst.dtype)
    vo_ref[...] = v_ref[...].astype(vo_ref.dtype)


def _rope(proj, pos, inv, tm=256):
    m = proj.shape[0]
    blk = lambda off: pl.BlockSpec((tm, A_W), lambda i: (i, off // A_W))
    out = pl.BlockSpec((tm, A_W), lambda i: (i, 0))
    shp = jax.ShapeDtypeStruct((m, A_W), BF16)
    return pl.pallas_call(
        _rope_kernel,
        out_shape=(shp, shp, shp),
        grid=(m // tm,),
        in_specs=[pl.BlockSpec((tm, 1), lambda i: (i, 0)),
                  pl.BlockSpec((1, LANES), lambda i: (0, 0)),
                  blk(OFF_AQ), blk(OFF_AK), blk(OFF_AV)],
        out_specs=(out, out, out),
        compiler_params=_params("parallel"),
        name="rope",
    )(pos, inv, proj, proj, proj)


def _attn_kernel(lam_ref, q_ref, k_ref, v_ref, nw_ref, o_ref, *, lam_init, tq):
    qi = pl.program_id(2)
    lv = lam_ref[...]
    lam = (jnp.exp(jnp.sum(lv[0:1] * lv[1:2], -1, keepdims=True))
           - jnp.exp(jnp.sum(lv[2:3] * lv[3:4], -1, keepdims=True)) + lam_init)

    q = q_ref[...] * (A_DH ** -0.5)
    lane = lax.broadcasted_iota(jnp.int32, q.shape, 1)
    zero = jnp.zeros_like(q)
    qq = jnp.concatenate([jnp.where(lane < A_DH, q, zero), jnp.where(lane >= A_DH, q, zero)], axis=0)
    qpos = qi * tq + (lax.broadcasted_iota(jnp.int32, (2 * tq, tq), 0) & (tq - 1))
    kofs = lax.broadcasted_iota(jnp.int32, (2 * tq, tq), 1)

    def body(j, carry):
        m, l, acc = carry
        r0 = pl.multiple_of(j * tq, tq)
        k = k_ref[pl.ds(r0, tq), :]
        v = v_ref[pl.ds(r0, tq), :]
        s = lax.dot_general(qq, k, NT_DIMS, preferred_element_type=F32)
        s = jnp.where(kofs + j * tq <= qpos, s, NEG_BIG)
        m_new = jnp.maximum(m, jnp.max(s, -1, keepdims=True))
        p = jnp.exp(s - m_new)
        a = jnp.exp(m - m_new)
        l = a * l + jnp.sum(p, -1, keepdims=True)
        acc = a * acc + jnp.dot(p.astype(BF16), v, preferred_element_type=F32)
        return m_new, l, acc

    init = (jnp.full((2 * tq, 1), NEG_BIG, F32), jnp.zeros((2 * tq, 1), F32),
            jnp.zeros((2 * tq, 2 * A_DH), F32))
    _, l, acc = lax.fori_loop(0, qi + 1, body, init)
    o = acc / l
    o = o[:tq] - lam * o[tq:]
    o_ref[...] = (_rms_norm(o, nw_ref[...]) * (1.0 - lam_init)).astype(o_ref.dtype)


def _attention(q, k, v, lam_vecs, norm_w, lam_init, batch, seq, tq=256):
    nq = seq // tq
    hw = 2 * A_DH
    return pl.pallas_call(
        functools.partial(_attn_kernel, lam_init=lam_init, tq=tq),
        out_shape=jax.ShapeDtypeStruct((batch * seq, A_W), BF16),
        grid=(batch, A_HEADS, nq),
        in_specs=[pl.BlockSpec((4, A_DH), lambda b, h, i: (0, 0)),
                  pl.BlockSpec((tq, hw), lambda b, h, i: (b * nq + i, h)),
                  pl.BlockSpec((seq, hw), lambda b, h, i: (b, h)),
                  pl.BlockSpec((seq, hw), lambda b, h, i: (b, h)),
                  pl.BlockSpec((1, hw), lambda b, h, i: (0, h))],
        out_specs=pl.BlockSpec((tq, hw), lambda b, h, i: (b * nq + i, h)),
        compiler_params=_params("parallel", "parallel", "arbitrary"),
        name="diff_attn",
    )(lam_vecs, q, k, v, norm_w.reshape(1, -1))


def _hgrn_kernel(lbl_ref, q_ref, f_ref, v_ref, og_ref, nw_ref, out_ref, qs, ks, bs, st_s, *, layer):
    blk = pl.program_id(2)
    tb = q_ref.shape[0]

    @pl.when(blk == 0)
    def _():
        st_s[...] = jnp.zeros_like(st_s)

    logits = lbl_ref[...]
    e = jnp.exp(logits - jnp.max(logits, 0, keepdims=True))
    p = e / jnp.sum(e, 0, keepdims=True)
    lb = jnp.sum(p[0:layer + 1], 0, keepdims=True) - p[0:1]

    f_g = lb + (1.0 - lb) * _sigmoid(f_ref[...])
    bs[...] = _chunk_cumsum(jnp.log(jnp.maximum(f_g, F_FLOOR)), CHUNK)
    ks[...] = 1.0 - f_g
    qv = q_ref[...]
    qs[...] = qv * _sigmoid(qv)

    row_c = lax.broadcasted_iota(jnp.int32, (CHUNK, G_DK), 0)
    row_s = lax.broadcasted_iota(jnp.int32, (SUB, G_DK), 0)

    def chunk_step(c, carry):
        r0 = pl.multiple_of(c * CHUNK, CHUNK)
        bc = bs[pl.ds(r0, CHUNK), :]
        qc = qs[pl.ds(r0, CHUNK), :]
        kc = ks[pl.ds(r0, CHUNK), :]
        vc = v_ref[pl.ds(r0, CHUNK), :]
        vb = vc.astype(BF16)
        st = st_s[...]

        o_inter = lax.dot_general((qc * jnp.exp(bc)).astype(BF16), st.astype(BF16), NT_DIMS,
                                  preferred_element_type=F32)
        outs = []
        for i in range(CHUNK // SUB):
            r = slice(i * SUB, (i + 1) * SUB)
            b_i, q_i, k_i, v_i = bc[r], qc[r], kc[r], vc[r]
            o_i = o_inter[r]
            if i > 0:
                c_i = b_i[0:1]
                q_sc = (q_i * jnp.exp(b_i - c_i)).astype(BF16)
                k_sc = jnp.where(row_c < i * SUB, kc * jnp.exp(jnp.minimum(c_i - bc, 0.0)), 0.0)
                a_off = lax.dot_general(q_sc, k_sc.astype(BF16), NT_DIMS, preferred_element_type=F32)
                o_i = o_i + jnp.dot(a_off.astype(BF16), vb, preferred_element_type=F32)
            for s in range(SUB):
                dec = jnp.exp(jnp.minimum(b_i - b_i[s:s + 1], 0.0))
                a_s = jnp.sum(jnp.where(row_s >= s, q_i * (k_i[s:s + 1] * dec), 0.0), -1, keepdims=True)
                o_i = o_i + a_s * v_i[s:s + 1]
            outs.append(o_i)
        o = jnp.concatenate(outs, axis=0)

        b_last = bc[CHUNK - 1:CHUNK]
        k_dec = (kc * jnp.exp(b_last - bc)).astype(BF16)
        st_s[...] = st * jnp.exp(b_last) + lax.dot_general(vb, k_dec, TN_DIMS,
                                                           preferred_element_type=F32)

        gg = og_ref[pl.ds(r0, CHUNK), :]
        hn = _rms_norm(o, nw_ref[...]) * (gg * _sigmoid(gg))
        out_ref[pl.ds(r0, CHUNK), :] = hn.astype(out_ref.dtype)
        return carry

    lax.fori_loop(0, tb // CHUNK, chunk_step, 0)


def _hgrn(proj, lb_logits, norm_w, layer, batch, seq, tb=512):
    nblk = seq // tb
    w = G_DK
    col = lambda off: (lambda b, h, s: (b * nblk + s, off // w + h))
    vec = pl.BlockSpec((1, w), lambda b, h, s: (0, h))
    return pl.pallas_call(
        functools.partial(_hgrn_kernel, layer=layer),
        out_shape=jax.ShapeDtypeStruct((batch * seq, G_WV), BF16),
        grid=(batch, G_HEADS, nblk),
        in_specs=[pl.BlockSpec((DEPTH, w), lambda b, h, s: (0, h)),
                  pl.BlockSpec((tb, w), col(OFF_GQ)),
                  pl.BlockSpec((tb, w), col(OFF_GF)),
                  pl.BlockSpec((tb, w), col(OFF_GI)),
                  pl.BlockSpec((tb, w), col(OFF_GG)),
                  vec],
        out_specs=pl.BlockSpec((tb, w), lambda b, h, s: (b * nblk + s, h)),
        scratch_shapes=[pltpu.VMEM((tb, w), F32), pltpu.VMEM((tb, w), F32), pltpu.VMEM((tb, w), F32),
                        pltpu.VMEM((w, w), F32)],
        compiler_params=_params("parallel", "parallel", "arbitrary"),
        name="hgrn2",
    )(lb_logits, proj, proj, proj, proj, norm_w.reshape(1, -1))


def _merge_kernel(hm_ref, ha_ref, hg_ref, pm_ref, pa_ref, pg_ref, g0_ref, g1_ref, g2_ref, o_ref):
    y = (_sigmoid(g0_ref[...]) * jnp.dot(hm_ref[...], pm_ref[...], preferred_element_type=F32)
         + _sigmoid(g1_ref[...]) * jnp.dot(ha_ref[...], pa_ref[...], preferred_element_type=F32)
         + _sigmoid(g2_ref[...]) * jnp.dot(hg_ref[...], pg_ref[...], preferred_element_type=F32))
    o_ref[...] = y.astype(o_ref.dtype)


def _merge(hm, ha, hg, p_m, p_a, p_g, proj, tm=512, tn=512):
    m = hm.shape[0]
    act = lambda w: pl.BlockSpec((tm, w), lambda i, j: (i, 0))
    wgt = lambda w: pl.BlockSpec((w, tn), lambda i, j: (0, j))
    gate = lambda br: pl.BlockSpec((tm, tn), lambda i, j: (i, (OFF_GATE + br * D_MODEL) // tn + j))
    return pl.pallas_call(
        _merge_kernel,
        out_shape=jax.ShapeDtypeStruct((m, D_MODEL), BF16),
        grid=(m // tm, D_MODEL // tn),
        in_specs=[act(M_W), act(A_W), act(G_WV), wgt(M_W), wgt(A_W), wgt(G_WV),
                  gate(0), gate(1), gate(2)],
        out_specs=pl.BlockSpec((tm, tn), lambda i, j: (i, j)),
        compiler_params=_params("parallel", "arbitrary"),
        name="merge",
    )(hm, ha, hg, p_m, p_a, p_g, proj, proj, proj)


def _outproj_kernel(y_ref, w_ref, h_ref, lw_ref, lb_ref, o_ref, obf_ref):
    mix = jnp.dot(y_ref[...], w_ref[...], preferred_element_type=F32)
    h = _layer_norm(ALPHA * h_ref[...] + mix, lw_ref[...], lb_ref[...])
    o_ref[...] = h
    obf_ref[...] = h.astype(BF16)


def _outproj_ln(y, w_out, h, ln_w, ln_b, tm=512):
    m, d = h.shape
    row = pl.BlockSpec((tm, d), lambda i: (i, 0))
    vec = pl.BlockSpec((1, d), lambda i: (0, 0))
    return pl.pallas_call(
        _outproj_kernel,
        out_shape=(jax.ShapeDtypeStruct((m, d), F32), jax.ShapeDtypeStruct((m, d), BF16)),
        grid=(m // tm,),
        in_specs=[row, pl.BlockSpec((d, d), lambda i: (0, 0)), row, vec, vec],
        out_specs=(row, row),
        compiler_params=_params("parallel"),
        name="outproj_ln",
    )(y, w_out, h, ln_w.reshape(1, d), ln_b.reshape(1, d))


def _mlp_kernel(hbf_ref, wu_ref, wd_ref, h_ref, lw_ref, lb_ref, o_ref, obf_ref, acc):
    j = pl.program_id(1)

    @pl.when(j == 0)
    def _():
        acc[...] = jnp.zeros_like(acc)

    up = jnp.maximum(jnp.dot(hbf_ref[...], wu_ref[...], preferred_element_type=F32), 0.0)
    acc[...] += jnp.dot((up * up).astype(BF16), wd_ref[...], preferred_element_type=F32)

    @pl.when(j == pl.num_programs(1) - 1)
    def _():
        h = _layer_norm(ALPHA * h_ref[...] + acc[...], lw_ref[...], lb_ref[...])
        o_ref[...] = h
        obf_ref[...] = h.astype(BF16)


def _mlp_ln(hbf, w_up, w_down, h, ln_w, ln_b, tm=512, tf=512):
    m, d = h.shape
    f = w_up.shape[1]
    row = pl.BlockSpec((tm, d), lambda i, j: (i, 0))
    vec = pl.BlockSpec((1, d), lambda i, j: (0, 0))
    return pl.pallas_call(
        _mlp_kernel,
        out_shape=(jax.ShapeDtypeStruct((m, d), F32), jax.ShapeDtypeStruct((m, d), BF16)),
        grid=(m // tm, f // tf),
        in_specs=[row, pl.BlockSpec((d, tf), lambda i, j: (0, j)),
                  pl.BlockSpec((tf, d), lambda i, j: (j, 0)), row, vec, vec],
        out_specs=(row, row),
        scratch_shapes=[pltpu.VMEM((tm, d), F32)],
        compiler_params=_params("parallel", "arbitrary"),
        name="mlp_ln",
    )(hbf, w_up, w_down, h, ln_w.reshape(1, d), ln_b.reshape(1, d))


def _split_w_in(w_in_l):
    g0 = OFF_AQ
    w_main = jnp.concatenate([w_in_l[:, :g0], w_in_l[:, g0 + 2 * M_HEADS:]], axis=1).astype(BF16)
    wg = jnp.zeros((D_MODEL, M_HEADS, LANES), F32)
    wg = wg.at[:, :, 0].set(w_in_l[:, g0:g0 + M_HEADS])
    wg = wg.at[:, :, 1].set(w_in_l[:, g0 + M_HEADS:g0 + 2 * M_HEADS])
    return w_main, wg.reshape(D_MODEL, M_HEADS * LANES).astype(BF16)


def _split_gate_b(gate_b_l):
    gb = jnp.zeros((M_HEADS, LANES), F32)
    gb = gb.at[:, 0].set(gate_b_l[:M_HEADS]).at[:, 1].set(gate_b_l[M_HEADS:])
    return gb.reshape(1, M_HEADS * LANES)


def kernel(x, positions, ln0_w, ln0_b, w_in, m_conv_w, m_conv_b, m_gate_b, m_norm_w, a_lambda, a_norm_w,
           g_lb_logits, g_norm_w, p_m, p_a, p_g, w_out, ln1_w, ln1_b, w_up, w_down, ln2_w, ln2_b):
    batch, seq, d = x.shape
    m = batch * seq
    pos = positions.reshape(m, 1)
    inv = jnp.power(jnp.float32(ROPE_THETA), -jnp.arange(0, ROT_DIM, 2, dtype=F32) / ROT_DIM)
    inv = jnp.tile(inv, LANES // inv.shape[0]).reshape(1, LANES)

    h, hbf = _ln(x.reshape(m, d), ln0_w, ln0_b)
    for l in range(DEPTH):
        w_main, w_gate = _split_w_in(w_in[l])
        proj = _matmul(hbf, w_main, F32, 1024, 512, "in_proj")
        gates = _matmul(hbf, w_gate, F32, 1024, M_HEADS * LANES, "gate_proj")

        hm = _mlstm(proj, gates, _split_gate_b(m_gate_b[l]), m_conv_w[l], m_conv_b[l], m_norm_w[l],
                    batch, seq)
        aq, ak, av = _rope(proj, pos, inv)
        lam_init = 0.8 - 0.6 * math.exp(-0.3 * l)
        ha = _attention(aq, ak, av, a_lambda[l], a_norm_w[l], lam_init, batch, seq)
        hg = _hgrn(proj, g_lb_logits, g_norm_w[l], l, batch, seq)

        y = _merge(hm, ha, hg, p_m[l].astype(BF16), p_a[l].astype(BF16), p_g[l].astype(BF16), proj)
        h, hbf = _outproj_ln(y, w_out[l].astype(BF16), h, ln1_w[l], ln1_b[l])
        h, hbf = _mlp_ln(hbf, w_up[l].astype(BF16), w_down[l].astype(BF16), h, ln2_w[l], ln2_b[l])
    return h.reshape(batch, seq, d)
```

```python
import functools
import math

import jax
import jax.numpy as jnp
from jax import lax
from jax.experimental import pallas as pl
from jax.experimental.pallas import tpu as pltpu

F32 = jnp.float32
BF16 = jnp.bfloat16

D_MODEL = 2048
DEPTH = 2
M_HEADS = 4
M_DH = 256
M_W = M_HEADS * M_DH
CONV_K = 4
A_HEADS = 8
A_DH = 64
A_W = A_HEADS * 2 * A_DH
ROT_DIM = A_DH // 4
ROPE_THETA = 500000.0
G_HEADS = 8
G_DK = 128
G_WK = G_HEADS * G_DK
G_WV = G_HEADS * G_DK
CHUNK = 64
M_CHUNK = 256
SUB = 16
FACTOR_SPREAD = 60.0
N_BRANCH = 3
D_FF = 4 * D_MODEL
ALPHA = (2 * DEPTH) ** 0.25
LN_EPS = 1e-5
NORM_EPS = 1e-6
NEG_BIG = -1e30
F_FLOOR = 1e-30

OFF_MQ, OFF_MK, OFF_MV, OFF_MO = 0, M_W, 2 * M_W, 3 * M_W
PROJ_M_W = 4 * M_W
OFF_AQ = 0
OFF_AK = OFF_AQ + A_W
OFF_AV = OFF_AK + A_W
OFF_GQ = OFF_AV + A_W
OFF_GF = OFF_GQ + G_WK
OFF_GI = OFF_GF + G_WK
OFF_GG = OFF_GI + G_WV
OFF_GATE = OFF_GG + G_WV
PROJ_R_W = OFF_GATE + N_BRANCH * D_MODEL
LANES = 128
VMEM_LIMIT = 48 * 1024 * 1024

NT_DIMS = (((1,), (1,)), ((), ()))
TN_DIMS = (((0,), (0,)), ((), ()))


def _params(*sem):
    return pltpu.CompilerParams(dimension_semantics=sem, vmem_limit_bytes=VMEM_LIMIT)


def _sigmoid(x):
    return 1.0 / (1.0 + jnp.exp(-x))


def _layer_norm(x, w, b):
    mu = jnp.mean(x, -1, keepdims=True)
    xc = x - mu
    var = jnp.mean(xc * xc, -1, keepdims=True)
    return xc * lax.rsqrt(var + LN_EPS) * w + b


def _rms_norm(x, w):
    return x * lax.rsqrt(jnp.mean(x * x, -1, keepdims=True) + NORM_EPS) * w


def _chunk_cumsum(x, chunk):
    row = lax.broadcasted_iota(jnp.int32, x.shape, 0) & (chunk - 1)
    sh = 1
    while sh < chunk:
        x = x + jnp.where(row >= sh, pltpu.roll(x, sh, axis=0), 0.0)
        sh *= 2
    return x


def _ln_kernel(x_ref, w_ref, b_ref, o_ref, obf_ref):
    y = _layer_norm(x_ref[...], w_ref[...], b_ref[...])
    o_ref[...] = y
    obf_ref[...] = y.astype(BF16)


def _ln(x, w, b, tm=256):
    m, d = x.shape
    return pl.pallas_call(
        _ln_kernel,
        out_shape=(jax.ShapeDtypeStruct((m, d), F32), jax.ShapeDtypeStruct((m, d), BF16)),
        grid=(m // tm,),
        in_specs=[pl.BlockSpec((tm, d), lambda i: (i, 0)),
                  pl.BlockSpec((1, d), lambda i: (0, 0)),
                  pl.BlockSpec((1, d), lambda i: (0, 0))],
        out_specs=(pl.BlockSpec((tm, d), lambda i: (i, 0)),
                   pl.BlockSpec((tm, d), lambda i: (i, 0))),
        compiler_params=_params("parallel"),
        name="ln0",
    )(x, w.reshape(1, d), b.reshape(1, d))


def _mm_kernel(a_ref, b_ref, o_ref):
    o_ref[...] = jnp.dot(a_ref[...], b_ref[...], preferred_element_type=F32).astype(o_ref.dtype)


def _matmul(a, b, out_dtype, tm, tn, name):
    m, k = a.shape
    n = b.shape[1]
    return pl.pallas_call(
        _mm_kernel,
        out_shape=jax.ShapeDtypeStruct((m, n), out_dtype),
        grid=(m // tm, n // tn),
        in_specs=[pl.BlockSpec((tm, k), lambda i, j: (i, 0)),
                  pl.BlockSpec((k, tn), lambda i, j: (0, j))],
        out_specs=pl.BlockSpec((tm, tn), lambda i, j: (i, j)),
        compiler_params=_params("parallel", "arbitrary"),
        name=name,
    )(a, b)


def _mlstm_kernel(q_ref, k_ref, v_ref, og_ref, g_ref, gb_ref, cw_ref, cb_ref, nw_ref, out_ref,
                  qpad, kpad, c_s, n_s, m_s):
    blk = pl.program_id(1)
    tb = q_ref.shape[0]

    @pl.when(blk == 0)
    def _():
        qpad[0:8, :] = jnp.zeros((8, M_W), F32)
        kpad[0:8, :] = jnp.zeros((8, M_W), F32)
        c_s[...] = jnp.zeros_like(c_s)
        n_s[...] = jnp.zeros_like(n_s)
        m_s[...] = jnp.zeros_like(m_s)

    @pl.when(blk > 0)
    def _():
        qpad[0:8, :] = qpad[tb:tb + 8, :]
        kpad[0:8, :] = kpad[tb:tb + 8, :]

    qpad[8:8 + tb, :] = q_ref[...]
    kpad[8:8 + tb, :] = k_ref[...]

    def conv_silu(pad, cols, wofs):
        wcols = slice(wofs + cols.start, wofs + cols.stop)
        y = cb_ref[:, wcols]
        for j in range(CONV_K):
            off = 8 - (CONV_K - 1) + j
            y = y + cw_ref[j:j + 1, wcols] * pad[off:off + tb, cols]
        return y * _sigmoid(y)

    ri = lax.broadcasted_iota(jnp.int32, (tb, tb), 0)
    ci = lax.broadcasted_iota(jnp.int32, (tb, tb), 1)
    causal = ri >= ci

    for h in range(M_HEADS):
        cols = slice(h * M_DH, (h + 1) * M_DH)
        qc = conv_silu(qpad, cols, 0)
        kc = conv_silu(kpad, cols, M_W) * (M_DH ** -0.5)
        vc = v_ref[:, cols].astype(BF16)
        qb = qc.astype(BF16)

        gb = g_ref[:, h * LANES:(h + 1) * LANES] + gb_ref[:, h * LANES:(h + 1) * LANES]
        log_f = jnp.minimum(gb, 0.0) - jnp.log(1.0 + jnp.exp(-jnp.abs(gb)))
        bcum = _chunk_cumsum(log_f, tb)
        lane = lax.broadcasted_iota(jnp.int32, gb.shape, 1)
        pc = jnp.where(lane == 0, gb, bcum)
        pt = pc.T
        i_col, b_col = pc[:, 0:1], pc[:, 1:2]
        i_row, b_row = pt[0:1, :], pt[1:2, :]
        m_prev = m_s[h]
        c_prev = c_s[h]
        n_prev = n_s[h]

        log_d = jnp.where(causal, b_col + (i_row - b_row), NEG_BIG)
        log_inter = b_col + m_prev
        m_t = jnp.maximum(log_inter, jnp.max(log_d, -1, keepdims=True))
        w_inter = jnp.exp(log_inter - m_t)
        s = lax.dot_general(qb, kc.astype(BF16), NT_DIMS, preferred_element_type=F32)
        s = s * jnp.exp(log_d - m_t)
        num = (jnp.dot(s.astype(BF16), vc, preferred_element_type=F32)
               + w_inter * jnp.dot(qb, c_prev.astype(BF16), preferred_element_type=F32))
        den = jnp.sum(s, -1, keepdims=True) + w_inter * jnp.sum(qc * n_prev, -1, keepdims=True)
        hh = num / jnp.maximum(jnp.abs(den), jnp.exp(-m_t))

        g = b_col[tb - 1:tb, :]
        log_w = g - b_col + i_col
        m_new = jnp.maximum(g + m_prev, jnp.max(log_w, 0, keepdims=True))
        decay = jnp.exp(g + m_prev - m_new)
        kw = kc * jnp.exp(log_w - m_new)
        c_s[h] = decay * c_prev + lax.dot_general(kw.astype(BF16), vc, TN_DIMS,
                                                  preferred_element_type=F32)
        n_s[h] = decay * n_prev + jnp.sum(kw, 0, keepdims=True)
        m_s[h] = m_new

        hn = _rms_norm(hh, nw_ref[:, cols]) * _sigmoid(og_ref[:, cols])
        out_ref[:, cols] = hn.astype(out_ref.dtype)


def _mlstm(proj, gates, gate_b, conv_w, conv_b, norm_w, batch, seq, tb=M_CHUNK):
    nblk = seq // tb
    col = lambda off: pl.BlockSpec((tb, M_W), lambda b, s: (b * nblk + s, off // M_W))
    full = lambda a: pl.BlockSpec(a.shape, lambda b, s: (0, 0))
    conv_b = conv_b.reshape(1, -1)
    norm_w = norm_w.reshape(1, -1)
    return pl.pallas_call(
        _mlstm_kernel,
        out_shape=jax.ShapeDtypeStruct((batch * seq, M_W), BF16),
        grid=(batch, nblk),
        in_specs=[col(OFF_MQ), col(OFF_MK), col(OFF_MV), col(OFF_MO),
                  pl.BlockSpec((tb, M_HEADS * LANES), lambda b, s: (b * nblk + s, 0)),
                  full(gate_b), full(conv_w), full(conv_b), full(norm_w)],
        out_specs=pl.BlockSpec((tb, M_W), lambda b, s: (b * nblk + s, 0)),
        scratch_shapes=[pltpu.VMEM((tb + 8, M_W), F32), pltpu.VMEM((tb + 8, M_W), F32),
                        pltpu.VMEM((M_HEADS, M_DH, M_DH), F32), pltpu.VMEM((M_HEADS, 1, M_DH), F32),
                        pltpu.VMEM((M_HEADS, 1, 1), F32)],
        compiler_params=_params("parallel", "arbitrary"),
        name="mlstm",
    )(proj, proj, proj, proj, gates, gate_b, conv_w, conv_b, norm_w)


def _rope_kernel(pos_ref, inv_ref, q_ref, k_ref, v_ref, qo_ref, ko_ref, vo_ref):
    ang = pos_ref[...].astype(F32) * inv_ref[...]
    lane = lax.broadcasted_iota(jnp.int32, ang.shape, 1) & (A_DH - 1)
    half = ROT_DIM // 2
    cos, sin = jnp.cos(ang), jnp.sin(ang)
    c_same = jnp.where(lane < ROT_DIM, cos, 1.0)
    c_lo = jnp.where((lane >= half) & (lane < ROT_DIM), sin, 0.0)
    c_hi = jnp.where(lane < half, -sin, 0.0)
    for src, dst in ((q_ref, qo_ref), (k_ref, ko_ref)):
        for g in range(A_W // LANES):
            sl = slice(g * LANES, (g + 1) * LANES)
            x = src[:, sl]
            y = (x * c_same + pltpu.roll(x, half, axis=1) * c_lo
                 + pltpu.roll(x, LANES - half, axis=1) * c_hi)
            dst[:, sl] = y.astype(dst.dtype)
    vo_ref[...] = v_ref[...].astype(vo_ref.dtype)


def _rope(proj, pos, inv, tm=256):
    m = proj.shape[0]
    blk = lambda off: pl.BlockSpec((tm, A_W), lambda i: (i, off // A_W))
    out = pl.BlockSpec((tm, A_W), lambda i: (i, 0))
    shp = jax.ShapeDtypeStruct((m, A_W), BF16)
    return pl.pallas_call(
        _rope_kernel,
        out_shape=(shp, shp, shp),
        grid=(m // tm,),
        in_specs=[pl.BlockSpec((tm, 1), lambda i: (i, 0)),
                  pl.BlockSpec((1, LANES), lambda i: (0, 0)),
                  blk(OFF_AQ), blk(OFF_AK), blk(OFF_AV)],
        out_specs=(out, out, out),
        compiler_params=_params("parallel"),
        name="rope",
    )(pos, inv, proj, proj, proj)


def _attn_kernel(lam_ref, q_ref, k_ref, v_ref, nw_ref, o_ref, vext, m_s, acc_s, *, lam_init, tq, strip):
    qi = pl.program_id(2)
    hw = 2 * A_DH
    lv = lam_ref[...]
    lam = (jnp.exp(jnp.sum(lv[0:1] * lv[1:2], -1, keepdims=True))
           - jnp.exp(jnp.sum(lv[2:3] * lv[3:4], -1, keepdims=True)) + lam_init)

    @pl.when(qi == 0)
    def _():
        vext[:, :hw] = v_ref[...]
        vext[:, hw:] = jnp.ones((vext.shape[0], hw), BF16)

    q = q_ref[...] * (A_DH ** -0.5)
    lane = lax.broadcasted_iota(jnp.int32, q.shape, 1)
    zero = jnp.zeros_like(q)
    qq = jnp.concatenate([jnp.where(lane < A_DH, q, zero), jnp.where(lane >= A_DH, q, zero)], axis=0)
    m_s[...] = jnp.full(m_s.shape, NEG_BIG, F32)
    acc_s[...] = jnp.zeros_like(acc_s)

    def block(j, diagonal):
        r0 = pl.multiple_of(j * tq, tq)
        k = k_ref[pl.ds(r0, tq), :]
        v = vext[pl.ds(r0, tq), :]
        for r in range(0, 2 * tq, strip):
            rows = slice(r, r + strip)
            s = lax.dot_general(qq[rows], k, NT_DIMS, preferred_element_type=F32)
            if diagonal:
                row = (lax.broadcasted_iota(jnp.int32, s.shape, 0) + r) & (tq - 1)
                col = lax.broadcasted_iota(jnp.int32, s.shape, 1)
                s = jnp.where(col <= row, s, NEG_BIG)
            m_prev = m_s[rows, :]
            m_new = jnp.maximum(m_prev, jnp.max(s, -1, keepdims=True))
            p = jnp.exp(s - jnp.concatenate([m_new] * (tq // LANES), axis=1))
            a = jnp.exp(m_prev - m_new)
            acc_s[rows, :] = (jnp.concatenate([a, a], axis=1) * acc_s[rows, :]
                              + jnp.dot(p.astype(BF16), v, preferred_element_type=F32))
            m_s[rows, :] = m_new

    def off_diagonal(j, carry):
        block(j, False)
        return carry

    lax.fori_loop(0, qi, off_diagonal, 0)
    block(qi, True)

    acc = acc_s[...]
    o = acc[:, :hw] / acc[:, hw:]
    o = o[:tq] - lam * o[tq:]
    o_ref[...] = (_rms_norm(o, nw_ref[...]) * (1.0 - lam_init)).astype(o_ref.dtype)


def _attention(q, k, v, lam_vecs, norm_w, lam_init, batch, seq, tq=512, strip=128):
    nq = seq // tq
    hw = 2 * A_DH
    return pl.pallas_call(
        functools.partial(_attn_kernel, lam_init=lam_init, tq=tq, strip=min(strip, 2 * tq)),
        out_shape=jax.ShapeDtypeStruct((batch * seq, A_W), BF16),
        grid=(batch, A_HEADS, nq),
        in_specs=[pl.BlockSpec((4, A_DH), lambda b, h, i: (0, 0)),
                  pl.BlockSpec((tq, hw), lambda b, h, i: (b * nq + i, h)),
                  pl.BlockSpec((seq, hw), lambda b, h, i: (b, h)),
                  pl.BlockSpec((seq, hw), lambda b, h, i: (b, h)),
                  pl.BlockSpec((1, hw), lambda b, h, i: (0, h))],
        out_specs=pl.BlockSpec((tq, hw), lambda b, h, i: (b * nq + i, h)),
        scratch_shapes=[pltpu.VMEM((seq, 2 * hw), BF16), pltpu.VMEM((2 * tq, LANES), F32),
                        pltpu.VMEM((2 * tq, 2 * hw), F32)],
        compiler_params=_params("parallel", "parallel", "arbitrary"),
        name="diff_attn",
    )(lam_vecs, q, k, v, norm_w.reshape(1, -1))


def _hgrn_kernel(lbl_ref, q_ref, f_ref, v_ref, og_ref, nw_ref, out_ref, qs, ks, bs, st_s, *, layer):
    blk = pl.program_id(2)
    tb = q_ref.shape[0]
    nchunk = tb // CHUNK
    mid = CHUNK // 2 - 1

    @pl.when(blk == 0)
    def _():
        st_s[...] = jnp.zeros_like(st_s)

    logits = lbl_ref[...]
    e = jnp.exp(logits - jnp.max(logits, 0, keepdims=True))
    p = e / jnp.sum(e, 0, keepdims=True)
    lb = jnp.sum(p[0:layer + 1], 0, keepdims=True) - p[0:1]

    f_g = lb + (1.0 - lb) * _sigmoid(f_ref[...])
    bs[...] = _chunk_cumsum(jnp.log(jnp.maximum(f_g, F_FLOOR)), CHUNK)
    ks[...] = 1.0 - f_g
    qv = q_ref[...]
    qs[...] = qv * _sigmoid(qv)

    def finish(r0, o):
        gg = og_ref[pl.ds(r0, CHUNK), :]
        hn = _rms_norm(o, nw_ref[...]) * (gg * _sigmoid(gg))
        out_ref[pl.ds(r0, CHUNK), :] = hn.astype(out_ref.dtype)

    def state_terms(bc, qc, kc, vb, st):
        b_last = bc[CHUNK - 1:CHUNK]
        o_inter = lax.dot_general((qc * jnp.exp(bc)).astype(BF16), st.astype(BF16), NT_DIMS,
                                  preferred_element_type=F32)
        k_dec = (kc * jnp.exp(b_last - bc)).astype(BF16)
        st_new = st * jnp.exp(b_last) + lax.dot_general(vb, k_dec, TN_DIMS, preferred_element_type=F32)
        return o_inter, st_new

    spread = jnp.zeros((1, G_DK), F32)
    for c in range(nchunk):
        b_mid = bs[c * CHUNK + mid:c * CHUNK + mid + 1, :]
        spread = jnp.maximum(spread, jnp.maximum(bs[c * CHUNK:c * CHUNK + 1, :] - b_mid,
                                                 b_mid - bs[(c + 1) * CHUNK - 1:(c + 1) * CHUNK, :]))
    narrow = jnp.max(spread) <= FACTOR_SPREAD

    @pl.when(narrow)
    def _():
        ri = lax.broadcasted_iota(jnp.int32, (CHUNK, CHUNK), 0)
        ci = lax.broadcasted_iota(jnp.int32, (CHUNK, CHUNK), 1)
        causal = ri >= ci
        st = st_s[...]
        for c in range(nchunk):
            r0 = c * CHUNK
            bc, qc, kc = bs[r0:r0 + CHUNK, :], qs[r0:r0 + CHUNK, :], ks[r0:r0 + CHUNK, :]
            vb = v_ref[r0:r0 + CHUNK, :].astype(BF16)
            b_mid = bc[mid:mid + 1]
            q_sc = (qc * jnp.exp(bc - b_mid)).astype(BF16)
            k_sc = (kc * jnp.exp(b_mid - bc)).astype(BF16)
            a = lax.dot_general(q_sc, k_sc, NT_DIMS, preferred_element_type=F32)
            a = jnp.where(causal, a, 0.0).astype(BF16)
            o_inter, st = state_terms(bc, qc, kc, vb, st)
            finish(r0, o_inter + jnp.dot(a, vb, preferred_element_type=F32))
        st_s[...] = st

    @pl.when(jnp.logical_not(narrow))
    def _():
        row_c = lax.broadcasted_iota(jnp.int32, (CHUNK, G_DK), 0)
        row_s = lax.broadcasted_iota(jnp.int32, (SUB, G_DK), 0)

        def chunk_step(c, carry):
            r0 = pl.multiple_of(c * CHUNK, CHUNK)
            bc = bs[pl.ds(r0, CHUNK), :]
            qc = qs[pl.ds(r0, CHUNK), :]
            kc = ks[pl.ds(r0, CHUNK), :]
            vc = v_ref[pl.ds(r0, CHUNK), :]
            vb = vc.astype(BF16)
            o_inter, st_new = state_terms(bc, qc, kc, vb, st_s[...])
            outs = []
            for i in range(CHUNK // SUB):
                r = slice(i * SUB, (i + 1) * SUB)
                b_i, q_i, k_i, v_i = bc[r], qc[r], kc[r], vc[r]
                o_i = o_inter[r]
                if i > 0:
                    c_i = b_i[0:1]
                    q_sc = (q_i * jnp.exp(b_i - c_i)).astype(BF16)
                    k_sc = jnp.where(row_c < i * SUB, kc * jnp.exp(jnp.minimum(c_i - bc, 0.0)), 0.0)
                    a_off = lax.dot_general(q_sc, k_sc.astype(BF16), NT_DIMS,
                                            preferred_element_type=F32)
                    o_i = o_i + jnp.dot(a_off.astype(BF16), vb, preferred_element_type=F32)
                for s in range(SUB):
                    dec = jnp.exp(jnp.minimum(b_i - b_i[s:s + 1], 0.0))
                    a_s = jnp.sum(jnp.where(row_s >= s, q_i * (k_i[s:s + 1] * dec), 0.0), -1,
                                  keepdims=True)
                    o_i = o_i + a_s * v_i[s:s + 1]
                outs.append(o_i)
            st_s[...] = st_new
            finish(r0, jnp.concatenate(outs, axis=0))
            return carry

        lax.fori_loop(0, nchunk, chunk_step, 0)


def _hgrn(proj, lb_logits, norm_w, layer, batch, seq, tb=512):
    nblk = seq // tb
    w = G_DK
    col = lambda off: (lambda b, h, s: (b * nblk + s, off // w + h))
    vec = pl.BlockSpec((1, w), lambda b, h, s: (0, h))
    return pl.pallas_call(
        functools.partial(_hgrn_kernel, layer=layer),
        out_shape=jax.ShapeDtypeStruct((batch * seq, G_WV), BF16),
        grid=(batch, G_HEADS, nblk),
        in_specs=[pl.BlockSpec((DEPTH, w), lambda b, h, s: (0, h)),
                  pl.BlockSpec((tb, w), col(OFF_GQ)),
                  pl.BlockSpec((tb, w), col(OFF_GF)),
                  pl.BlockSpec((tb, w), col(OFF_GI)),
                  pl.BlockSpec((tb, w), col(OFF_GG)),
                  vec],
        out_specs=pl.BlockSpec((tb, w), lambda b, h, s: (b * nblk + s, h)),
        scratch_shapes=[pltpu.VMEM((tb, w), F32), pltpu.VMEM((tb, w), F32), pltpu.VMEM((tb, w), F32),
                        pltpu.VMEM((w, w), F32)],
        compiler_params=_params("parallel", "parallel", "arbitrary"),
        name="hgrn2",
    )(lb_logits, proj, proj, proj, proj, norm_w.reshape(1, -1))


def _merge_kernel(hm_ref, ha_ref, hg_ref, pm_ref, pa_ref, pg_ref, g0_ref, g1_ref, g2_ref, o_ref):
    y = (_sigmoid(g0_ref[...]) * jnp.dot(hm_ref[...], pm_ref[...], preferred_element_type=F32)
         + _sigmoid(g1_ref[...]) * jnp.dot(ha_ref[...], pa_ref[...], preferred_element_type=F32)
         + _sigmoid(g2_ref[...]) * jnp.dot(hg_ref[...], pg_ref[...], preferred_element_type=F32))
    o_ref[...] = y.astype(o_ref.dtype)


def _merge(hm, ha, hg, p_m, p_a, p_g, proj, tm=512, tn=512):
    m = hm.shape[0]
    act = lambda w: pl.BlockSpec((tm, w), lambda i, j: (i, 0))
    wgt = lambda w: pl.BlockSpec((w, tn), lambda i, j: (0, j))
    gate = lambda br: pl.BlockSpec((tm, tn), lambda i, j: (i, (OFF_GATE + br * D_MODEL) // tn + j))
    return pl.pallas_call(
        _merge_kernel,
        out_shape=jax.ShapeDtypeStruct((m, D_MODEL), BF16),
        grid=(m // tm, D_MODEL // tn),
        in_specs=[act(M_W), act(A_W), act(G_WV), wgt(M_W), wgt(A_W), wgt(G_WV),
                  gate(0), gate(1), gate(2)],
        out_specs=pl.BlockSpec((tm, tn), lambda i, j: (i, j)),
        compiler_params=_params("parallel", "arbitrary"),
        name="merge",
    )(hm, ha, hg, p_m, p_a, p_g, proj, proj, proj)


def _outproj_kernel(y_ref, w_ref, h_ref, lw_ref, lb_ref, o_ref, obf_ref):
    mix = jnp.dot(y_ref[...], w_ref[...], preferred_element_type=F32)
    h = _layer_norm(ALPHA * h_ref[...] + mix, lw_ref[...], lb_ref[...])
    o_ref[...] = h
    obf_ref[...] = h.astype(BF16)


def _outproj_ln(y, w_out, h, ln_w, ln_b, tm=512):
    m, d = h.shape
    row = pl.BlockSpec((tm, d), lambda i: (i, 0))
    vec = pl.BlockSpec((1, d), lambda i: (0, 0))
    return pl.pallas_call(
        _outproj_kernel,
        out_shape=(jax.ShapeDtypeStruct((m, d), F32), jax.ShapeDtypeStruct((m, d), BF16)),
        grid=(m // tm,),
        in_specs=[row, pl.BlockSpec((d, d), lambda i: (0, 0)), row, vec, vec],
        out_specs=(row, row),
        compiler_params=_params("parallel"),
        name="outproj_ln",
    )(y, w_out, h, ln_w.reshape(1, d), ln_b.reshape(1, d))


def _mlp_kernel(hbf_ref, wu_ref, wd_ref, h_ref, lw_ref, lb_ref, o_ref, obf_ref, acc):
    j = pl.program_id(1)

    @pl.when(j == 0)
    def _():
        acc[...] = jnp.zeros_like(acc)

    up = jnp.maximum(jnp.dot(hbf_ref[...], wu_ref[...], preferred_element_type=F32), 0.0)
    acc[...] += jnp.dot((up * up).astype(BF16), wd_ref[...], preferred_element_type=F32)

    @pl.when(j == pl.num_programs(1) - 1)
    def _():
        h = _layer_norm(ALPHA * h_ref[...] + acc[...], lw_ref[...], lb_ref[...])
        o_ref[...] = h
        obf_ref[...] = h.astype(BF16)


def _mlp_ln(hbf, w_up, w_down, h, ln_w, ln_b, tm=512, tf=512):
    m, d = h.shape
    f = w_up.shape[1]
    row = pl.BlockSpec((tm, d), lambda i, j: (i, 0))
    vec = pl.BlockSpec((1, d), lambda i, j: (0, 0))
    return pl.pallas_call(
        _mlp_kernel,
        out_shape=(jax.ShapeDtypeStruct((m, d), F32), jax.ShapeDtypeStruct((m, d), BF16)),
        grid=(m // tm, f // tf),
        in_specs=[row, pl.BlockSpec((d, tf), lambda i, j: (0, j)),
                  pl.BlockSpec((tf, d), lambda i, j: (j, 0)), row, vec, vec],
        out_specs=(row, row),
        scratch_shapes=[pltpu.VMEM((tm, d), F32)],
        compiler_params=_params("parallel", "arbitrary"),
        name="mlp_ln",
    )(hbf, w_up, w_down, h, ln_w.reshape(1, d), ln_b.reshape(1, d))


def _split_w_in(w_in_l):
    g0 = PROJ_M_W
    w_m = w_in_l[:, :g0].astype(BF16)
    w_r = w_in_l[:, g0 + 2 * M_HEADS:].astype(BF16)
    wg = jnp.zeros((D_MODEL, M_HEADS, LANES), F32)
    wg = wg.at[:, :, 0].set(w_in_l[:, g0:g0 + M_HEADS])
    wg = wg.at[:, :, 1].set(w_in_l[:, g0 + M_HEADS:g0 + 2 * M_HEADS])
    return w_m, w_r, wg.reshape(D_MODEL, M_HEADS * LANES).astype(BF16)


def _split_gate_b(gate_b_l):
    gb = jnp.zeros((M_HEADS, LANES), F32)
    gb = gb.at[:, 0].set(gate_b_l[:M_HEADS]).at[:, 1].set(gate_b_l[M_HEADS:])
    return gb.reshape(1, M_HEADS * LANES)


def kernel(x, positions, ln0_w, ln0_b, w_in, m_conv_w, m_conv_b, m_gate_b, m_norm_w, a_lambda, a_norm_w,
           g_lb_logits, g_norm_w, p_m, p_a, p_g, w_out, ln1_w, ln1_b, w_up, w_down, ln2_w, ln2_b):
    batch, seq, d = x.shape
    m = batch * seq
    pos = positions.reshape(m, 1)
    inv = jnp.power(jnp.float32(ROPE_THETA), -jnp.arange(0, ROT_DIM, 2, dtype=F32) / ROT_DIM)
    inv = jnp.tile(inv, LANES // inv.shape[0]).reshape(1, LANES)

    h, hbf = _ln(x.reshape(m, d), ln0_w, ln0_b)
    for l in range(DEPTH):
        w_m, w_r, w_gate = _split_w_in(w_in[l])
        proj_m = _matmul(hbf, w_m, F32, 1024, 512, "in_proj_m")
        proj_r = _matmul(hbf, w_r, F32, 1024, 512, "in_proj_r")
        gates = _matmul(hbf, w_gate, F32, 1024, M_HEADS * LANES, "gate_proj")

        hm = _mlstm(proj_m, gates, _split_gate_b(m_gate_b[l]), m_conv_w[l], m_conv_b[l], m_norm_w[l],
                    batch, seq)
        aq, ak, av = _rope(proj_r, pos, inv)
        lam_init = 0.8 - 0.6 * math.exp(-0.3 * l)
        ha = _attention(aq, ak, av, a_lambda[l], a_norm_w[l], lam_init, batch, seq)
        hg = _hgrn(proj_r, g_lb_logits, g_norm_w[l], l, batch, seq)

        y = _merge(hm, ha, hg, p_m[l].astype(BF16), p_a[l].astype(BF16), p_g[l].astype(BF16), proj_r)
        h, hbf = _outproj_ln(y, w_out[l].astype(BF16), h, ln1_w[l], ln1_b[l])
        h, hbf = _mlp_ln(hbf, w_up[l].astype(BF16), w_down[l].astype(BF16), h, ln2_w[l], ln2_b[l])
    return h.reshape(batch, seq, d)
```

```python
import functools
import math

import jax
import jax.numpy as jnp
from jax import lax
from jax.experimental import pallas as pl
from jax.experimental.pallas import tpu as pltpu

F32 = jnp.float32
BF16 = jnp.bfloat16

D_MODEL = 2048
DEPTH = 2
M_HEADS = 4
M_DH = 256
M_W = M_HEADS * M_DH
CONV_K = 4
A_HEADS = 8
A_DH = 64
A_W = A_HEADS * 2 * A_DH
ROT_DIM = A_DH // 4
ROPE_THETA = 500000.0
G_HEADS = 8
G_DK = 128
G_WK = G_HEADS * G_DK
G_WV = G_HEADS * G_DK
CHUNK = 64
M_CHUNK = 256
SUB = 16
FACTOR_SPREAD = 60.0
N_BRANCH = 3
D_FF = 4 * D_MODEL
ALPHA = (2 * DEPTH) ** 0.25
LN_EPS = 1e-5
NORM_EPS = 1e-6
NEG_BIG = -1e30
F_FLOOR = 1e-30

OFF_MQ, OFF_MK, OFF_MV, OFF_MO = 0, M_W, 2 * M_W, 3 * M_W
PROJ_M_W = 4 * M_W
OFF_AQ = 0
OFF_AK = OFF_AQ + A_W
OFF_AV = OFF_AK + A_W
OFF_GQ = OFF_AV + A_W
OFF_GF = OFF_GQ + G_WK
OFF_GI = OFF_GF + G_WK
OFF_GG = OFF_GI + G_WV
OFF_GATE = OFF_GG + G_WV
PROJ_R_W = OFF_GATE + N_BRANCH * D_MODEL
LANES = 128
VMEM_LIMIT = 48 * 1024 * 1024

NT_DIMS = (((1,), (1,)), ((), ()))
TN_DIMS = (((0,), (0,)), ((), ()))


def _params(*sem):
    return pltpu.CompilerParams(dimension_semantics=sem, vmem_limit_bytes=VMEM_LIMIT)


def _sigmoid(x):
    return 1.0 / (1.0 + jnp.exp(-x))


def _gate_sigmoid(x):
    return 0.5 * jnp.tanh(0.5 * x) + 0.5


def _layer_norm(x, w, b):
    mu = jnp.mean(x, -1, keepdims=True)
    xc = x - mu
    var = jnp.mean(xc * xc, -1, keepdims=True)
    return xc * lax.rsqrt(var + LN_EPS) * w + b


def _rms_norm(x, w):
    return x * lax.rsqrt(jnp.mean(x * x, -1, keepdims=True) + NORM_EPS) * w


def _chunk_cumsum(x, chunk):
    row = lax.broadcasted_iota(jnp.int32, x.shape, 0) & (chunk - 1)
    sh = 1
    while sh < chunk:
        x = x + jnp.where(row >= sh, pltpu.roll(x, sh, axis=0), 0.0)
        sh *= 2
    return x


def _ln_kernel(x_ref, w_ref, b_ref, o_ref, obf_ref):
    y = _layer_norm(x_ref[...], w_ref[...], b_ref[...])
    o_ref[...] = y
    obf_ref[...] = y.astype(BF16)


def _ln(x, w, b, tm=256):
    m, d = x.shape
    return pl.pallas_call(
        _ln_kernel,
        out_shape=(jax.ShapeDtypeStruct((m, d), F32), jax.ShapeDtypeStruct((m, d), BF16)),
        grid=(m // tm,),
        in_specs=[pl.BlockSpec((tm, d), lambda i: (i, 0)),
                  pl.BlockSpec((1, d), lambda i: (0, 0)),
                  pl.BlockSpec((1, d), lambda i: (0, 0))],
        out_specs=(pl.BlockSpec((tm, d), lambda i: (i, 0)),
                   pl.BlockSpec((tm, d), lambda i: (i, 0))),
        compiler_params=_params("parallel"),
        name="ln0",
    )(x, w.reshape(1, d), b.reshape(1, d))


def _mm_kernel(a_ref, b_ref, o_ref):
    o_ref[...] = jnp.dot(a_ref[...], b_ref[...], preferred_element_type=F32).astype(o_ref.dtype)


def _matmul(a, w, layer, col0, n, out_dtype, tm, tn, name):
    m, k = a.shape
    return pl.pallas_call(
        _mm_kernel,
        out_shape=jax.ShapeDtypeStruct((m, n), out_dtype),
        grid=(m // tm, n // tn),
        in_specs=[pl.BlockSpec((tm, k), lambda i, j: (i, 0)),
                  pl.BlockSpec((None, k, tn), lambda i, j: (layer, 0, col0 // tn + j))],
        out_specs=pl.BlockSpec((tm, tn), lambda i, j: (i, j)),
        compiler_params=_params("parallel", "arbitrary"),
        name=name,
    )(a, w)


def _mlstm_kernel(q_ref, k_ref, v_ref, og_ref, g_ref, gb_ref, cw_ref, cb_ref, nw_ref, out_ref,
                  qpad, kpad, c_s, n_s, m_s):
    blk = pl.program_id(1)
    tb = q_ref.shape[0]

    @pl.when(blk == 0)
    def _():
        qpad[0:8, :] = jnp.zeros((8, M_W), F32)
        kpad[0:8, :] = jnp.zeros((8, M_W), F32)
        c_s[...] = jnp.zeros_like(c_s)
        n_s[...] = jnp.zeros_like(n_s)
        m_s[...] = jnp.zeros_like(m_s)

    @pl.when(blk > 0)
    def _():
        qpad[0:8, :] = qpad[tb:tb + 8, :]
        kpad[0:8, :] = kpad[tb:tb + 8, :]

    qpad[8:8 + tb, :] = q_ref[...].astype(F32)
    kpad[8:8 + tb, :] = k_ref[...].astype(F32)

    def conv_silu(pad, cols, wofs):
        wcols = slice(wofs + cols.start, wofs + cols.stop)
        y = cb_ref[:, wcols]
        for j in range(CONV_K):
            off = 8 - (CONV_K - 1) + j
            y = y + cw_ref[j:j + 1, wcols] * pad[off:off + tb, cols]
        return y * _gate_sigmoid(y)

    ri = lax.broadcasted_iota(jnp.int32, (tb, tb), 0)
    ci = lax.broadcasted_iota(jnp.int32, (tb, tb), 1)
    causal = ri >= ci

    gb = g_ref[...] + gb_ref[...]
    log_f = jnp.minimum(gb, 0.0) - jnp.log(1.0 + jnp.exp(-jnp.abs(gb)))
    bcum = _chunk_cumsum(log_f, tb)
    lane = lax.broadcasted_iota(jnp.int32, gb.shape, 1)
    pc = jnp.where(lane < M_HEADS, gb, bcum)
    pt = pc.T

    for h in range(M_HEADS):
        cols = slice(h * M_DH, (h + 1) * M_DH)
        qc = conv_silu(qpad, cols, 0)
        kc = conv_silu(kpad, cols, M_W) * (M_DH ** -0.5)
        vc = v_ref[:, cols]
        qb = qc.astype(BF16)

        i_col, b_col = pc[:, h:h + 1], pc[:, M_HEADS + h:M_HEADS + h + 1]
        i_row, b_row = pt[h:h + 1, :], pt[M_HEADS + h:M_HEADS + h + 1, :]
        m_prev = m_s[h]
        c_prev = c_s[h]
        n_prev = n_s[h]

        log_d = jnp.where(causal, b_col + (i_row - b_row), NEG_BIG)
        log_inter = b_col + m_prev
        m_t = jnp.maximum(log_inter, jnp.max(log_d, -1, keepdims=True))
        w_inter = jnp.exp(log_inter - m_t)
        s = lax.dot_general(qb, kc.astype(BF16), NT_DIMS, preferred_element_type=F32)
        s = s * jnp.exp(log_d - m_t)
        num = (jnp.dot(s.astype(BF16), vc, preferred_element_type=F32)
               + w_inter * jnp.dot(qb, c_prev.astype(BF16), preferred_element_type=F32))
        den = jnp.sum(s, -1, keepdims=True) + w_inter * jnp.sum(qc * n_prev, -1, keepdims=True)
        hh = num / jnp.maximum(jnp.abs(den), jnp.exp(-m_t))

        g = b_col[tb - 1:tb, :]
        log_w = g - b_col + i_col
        m_new = jnp.maximum(g + m_prev, jnp.max(log_w, 0, keepdims=True))
        decay = jnp.exp(g + m_prev - m_new)
        kw = kc * jnp.exp(log_w - m_new)
        c_s[h] = decay * c_prev + lax.dot_general(kw.astype(BF16), vc, TN_DIMS,
                                                  preferred_element_type=F32)
        n_s[h] = decay * n_prev + jnp.sum(kw, 0, keepdims=True)
        m_s[h] = m_new

        hn = _rms_norm(hh, nw_ref[:, cols]) * _gate_sigmoid(og_ref[:, cols].astype(F32))
        out_ref[:, cols] = hn.astype(out_ref.dtype)


def _mlstm(proj, gates, gate_b, conv_w, conv_b, norm_w, batch, seq, tb=M_CHUNK):
    nblk = seq // tb
    col = lambda off: pl.BlockSpec((tb, M_W), lambda b, s: (b * nblk + s, off // M_W))
    full = lambda a: pl.BlockSpec(a.shape, lambda b, s: (0, 0))
    conv_b = conv_b.reshape(1, -1)
    norm_w = norm_w.reshape(1, -1)
    return pl.pallas_call(
        _mlstm_kernel,
        out_shape=jax.ShapeDtypeStruct((batch * seq, M_W), BF16),
        grid=(batch, nblk),
        in_specs=[col(OFF_MQ), col(OFF_MK), col(OFF_MV), col(OFF_MO),
                  pl.BlockSpec((tb, LANES), lambda b, s: (b * nblk + s, 0)),
                  full(gate_b), full(conv_w), full(conv_b), full(norm_w)],
        out_specs=pl.BlockSpec((tb, M_W), lambda b, s: (b * nblk + s, 0)),
        scratch_shapes=[pltpu.VMEM((tb + 8, M_W), F32), pltpu.VMEM((tb + 8, M_W), F32),
                        pltpu.VMEM((M_HEADS, M_DH, M_DH), F32), pltpu.VMEM((M_HEADS, 1, M_DH), F32),
                        pltpu.VMEM((M_HEADS, 1, 1), F32)],
        compiler_params=_params("parallel", "arbitrary"),
        name="mlstm",
    )(proj, proj, proj, proj, gates, gate_b, conv_w, conv_b, norm_w)


def _rope_kernel(pos_ref, inv_ref, q_ref, k_ref, qo_ref, ko_ref):
    ang = pos_ref[...].astype(F32) * inv_ref[...]
    lane = lax.broadcasted_iota(jnp.int32, ang.shape, 1) & (A_DH - 1)
    half = ROT_DIM // 2
    cos, sin = jnp.cos(ang), jnp.sin(ang)
    c_same = jnp.where(lane < ROT_DIM, cos, 1.0)
    c_lo = jnp.where((lane >= half) & (lane < ROT_DIM), sin, 0.0)
    c_hi = jnp.where(lane < half, -sin, 0.0)
    for src, dst in ((q_ref, qo_ref), (k_ref, ko_ref)):
        for g in range(A_W // LANES):
            sl = slice(g * LANES, (g + 1) * LANES)
            x = src[:, sl].astype(F32)
            y = (x * c_same + pltpu.roll(x, half, axis=1) * c_lo
                 + pltpu.roll(x, LANES - half, axis=1) * c_hi)
            dst[:, sl] = y.astype(dst.dtype)


def _rope(proj, pos, inv, tm=256):
    m = proj.shape[0]
    blk = lambda off: pl.BlockSpec((tm, A_W), lambda i: (i, off // A_W))
    out = pl.BlockSpec((tm, A_W), lambda i: (i, 0))
    shp = jax.ShapeDtypeStruct((m, A_W), BF16)
    return pl.pallas_call(
        _rope_kernel,
        out_shape=(shp, shp),
        grid=(m // tm,),
        in_specs=[pl.BlockSpec((tm, 1), lambda i: (i, 0)),
                  pl.BlockSpec((1, LANES), lambda i: (0, 0)),
                  blk(OFF_AQ), blk(OFF_AK)],
        out_specs=(out, out),
        compiler_params=_params("parallel"),
        name="rope",
    )(pos, inv, proj, proj)


def _attn_kernel(lam_ref, q_ref, k_ref, v_ref, nw_ref, o_ref, vext, s_buf, m_s, acc_s, *,
                 lam_init, tq, strip):
    qi = pl.program_id(2)
    hw = 2 * A_DH
    lv = lam_ref[...]
    lam = (jnp.exp(jnp.sum(lv[0:1] * lv[1:2], -1, keepdims=True))
           - jnp.exp(jnp.sum(lv[2:3] * lv[3:4], -1, keepdims=True)) + lam_init)

    @pl.when(qi == 0)
    def _():
        vext[:, :hw] = v_ref[...]
        vext[:, hw:] = jnp.ones((vext.shape[0], hw), BF16)

    q = q_ref[...] * (A_DH ** -0.5)
    lane = lax.broadcasted_iota(jnp.int32, q.shape, 1)
    zero = jnp.zeros_like(q)
    qq = jnp.concatenate([jnp.where(lane < A_DH, q, zero), jnp.where(lane >= A_DH, q, zero)], axis=0)
    m_s[...] = jnp.full(m_s.shape, NEG_BIG, F32)
    acc_s[...] = jnp.zeros_like(acc_s)

    strips = [slice(r, r + strip) for r in range(0, 2 * tq, strip)]

    def scores(j, rows):
        r0 = pl.multiple_of(j * tq, tq)
        return lax.dot_general(qq[rows], k_ref[pl.ds(r0, tq), :], NT_DIMS, preferred_element_type=F32)

    def block(j, diagonal):
        r0 = pl.multiple_of(j * tq, tq)
        v = vext[pl.ds(r0, tq), :]
        for rows in strips:
            if diagonal:
                nk = (rows.start & (tq - 1)) + strip
                s = s_buf[rows, :nk]
                row = (lax.broadcasted_iota(jnp.int32, s.shape, 0) + rows.start) & (tq - 1)
                col = lax.broadcasted_iota(jnp.int32, s.shape, 1)
                s = jnp.where(col <= row, s, NEG_BIG)
            else:
                nk = tq
                s = s_buf[rows, :]
                s_buf[rows, :] = scores(j + 1, rows)
            m_prev = m_s[rows, :]
            m_new = jnp.maximum(m_prev, jnp.max(s, -1, keepdims=True))
            p = jnp.exp(s - jnp.concatenate([m_new] * (nk // LANES), axis=1))
            a = jnp.exp(m_prev - m_new)
            acc_s[rows, :] = (jnp.concatenate([a, a], axis=1) * acc_s[rows, :]
                              + jnp.dot(p.astype(BF16), v[:nk], preferred_element_type=F32))
            m_s[rows, :] = m_new

    def off_diagonal(j, carry):
        block(j, False)
        return carry

    for rows in strips:
        s_buf[rows, :] = scores(0, rows)
    lax.fori_loop(0, qi, off_diagonal, 0)
    block(qi, True)

    acc = acc_s[...]
    o = acc[:, :hw] / acc[:, hw:]
    o = o[:tq] - lam * o[tq:]
    o_ref[...] = (_rms_norm(o, nw_ref[...]) * (1.0 - lam_init)).astype(o_ref.dtype)


def _attention(q, k, proj, lam_vecs, norm_w, lam_init, batch, seq, tq=512, strip=128):
    nq = seq // tq
    hw = 2 * A_DH
    return pl.pallas_call(
        functools.partial(_attn_kernel, lam_init=lam_init, tq=tq, strip=min(strip, 2 * tq)),
        out_shape=jax.ShapeDtypeStruct((batch * seq, A_W), BF16),
        grid=(batch, A_HEADS, nq),
        in_specs=[pl.BlockSpec((4, A_DH), lambda b, h, i: (0, 0)),
                  pl.BlockSpec((tq, hw), lambda b, h, i: (b * nq + i, h)),
                  pl.BlockSpec((seq, hw), lambda b, h, i: (b, h)),
                  pl.BlockSpec((seq, hw), lambda b, h, i: (b, OFF_AV // hw + h)),
                  pl.BlockSpec((1, hw), lambda b, h, i: (0, h))],
        out_specs=pl.BlockSpec((tq, hw), lambda b, h, i: (b * nq + i, h)),
        scratch_shapes=[pltpu.VMEM((seq, 2 * hw), BF16), pltpu.VMEM((2 * tq, tq), F32),
                        pltpu.VMEM((2 * tq, LANES), F32), pltpu.VMEM((2 * tq, 2 * hw), F32)],
        compiler_params=_params("parallel", "parallel", "arbitrary"),
        name="diff_attn",
    )(lam_vecs, q, k, proj, norm_w.reshape(1, -1))


def _hgrn_kernel(lbl_ref, q_ref, f_ref, v_ref, og_ref, nw_ref, out_ref, qs, ks, bs, st_s, *, layer):
    blk = pl.program_id(2)
    tb = q_ref.shape[0]
    nchunk = tb // CHUNK
    mid = CHUNK // 2 - 1

    @pl.when(blk == 0)
    def _():
        st_s[...] = jnp.zeros_like(st_s)

    logits = lbl_ref[...]
    e = jnp.exp(logits - jnp.max(logits, 0, keepdims=True))
    p = e / jnp.sum(e, 0, keepdims=True)
    lb = jnp.sum(p[0:layer + 1], 0, keepdims=True) - p[0:1]

    f_g = lb + (1.0 - lb) * _sigmoid(f_ref[...].astype(F32))
    bs[...] = _chunk_cumsum(jnp.log(jnp.maximum(f_g, F_FLOOR)), CHUNK)
    ks[...] = 1.0 - f_g
    qv = q_ref[...].astype(F32)
    qs[...] = qv * _gate_sigmoid(qv)

    def finish(r0, o):
        gg = og_ref[pl.ds(r0, CHUNK), :].astype(F32)
        hn = _rms_norm(o, nw_ref[...]) * (gg * _gate_sigmoid(gg))
        out_ref[pl.ds(r0, CHUNK), :] = hn.astype(out_ref.dtype)

    def state_terms(bc, qc, kc, vb, st):
        b_last = bc[CHUNK - 1:CHUNK]
        o_inter = lax.dot_general((qc * jnp.exp(bc)).astype(BF16), st.astype(BF16), NT_DIMS,
                                  preferred_element_type=F32)
        k_dec = (kc * jnp.exp(b_last - bc)).astype(BF16)
        st_new = st * jnp.exp(b_last) + lax.dot_general(vb, k_dec, TN_DIMS, preferred_element_type=F32)
        return o_inter, st_new

    spread = jnp.zeros((1, G_DK), F32)
    for c in range(nchunk):
        b_mid = bs[c * CHUNK + mid:c * CHUNK + mid + 1, :]
        spread = jnp.maximum(spread, jnp.maximum(bs[c * CHUNK:c * CHUNK + 1, :] - b_mid,
                                                 b_mid - bs[(c + 1) * CHUNK - 1:(c + 1) * CHUNK, :]))
    narrow = jnp.max(spread) <= FACTOR_SPREAD

    @pl.when(narrow)
    def _():
        ri = lax.broadcasted_iota(jnp.int32, (CHUNK, CHUNK), 0)
        ci = lax.broadcasted_iota(jnp.int32, (CHUNK, CHUNK), 1)
        causal = ri >= ci
        st = st_s[...]
        for c in range(nchunk):
            r0 = c * CHUNK
            bc, qc, kc = bs[r0:r0 + CHUNK, :], qs[r0:r0 + CHUNK, :], ks[r0:r0 + CHUNK, :]
            vb = v_ref[r0:r0 + CHUNK, :]
            b_mid = bc[mid:mid + 1]
            q_sc = (qc * jnp.exp(bc - b_mid)).astype(BF16)
            k_sc = (kc * jnp.exp(b_mid - bc)).astype(BF16)
            a = lax.dot_general(q_sc, k_sc, NT_DIMS, preferred_element_type=F32)
            a = jnp.where(causal, a, 0.0).astype(BF16)
            o_inter, st = state_terms(bc, qc, kc, vb, st)
            finish(r0, o_inter + jnp.dot(a, vb, preferred_element_type=F32))
        st_s[...] = st

    @pl.when(jnp.logical_not(narrow))
    def _():
        row_c = lax.broadcasted_iota(jnp.int32, (CHUNK, G_DK), 0)
        row_s = lax.broadcasted_iota(jnp.int32, (SUB, G_DK), 0)

        def chunk_step(c, carry):
            r0 = pl.multiple_of(c * CHUNK, CHUNK)
            bc = bs[pl.ds(r0, CHUNK), :]
            qc = qs[pl.ds(r0, CHUNK), :]
            kc = ks[pl.ds(r0, CHUNK), :]
            vb = v_ref[pl.ds(r0, CHUNK), :]
            vc = vb.astype(F32)
            o_inter, st_new = state_terms(bc, qc, kc, vb, st_s[...])
            outs = []
            for i in range(CHUNK // SUB):
                r = slice(i * SUB, (i + 1) * SUB)
                b_i, q_i, k_i, v_i = bc[r], qc[r], kc[r], vc[r]
                o_i = o_inter[r]
                if i > 0:
                    c_i = b_i[0:1]
                    q_sc = (q_i * jnp.exp(b_i - c_i)).astype(BF16)
                    k_sc = jnp.where(row_c < i * SUB, kc * jnp.exp(jnp.minimum(c_i - bc, 0.0)), 0.0)
                    a_off = lax.dot_general(q_sc, k_sc.astype(BF16), NT_DIMS,
                                            preferred_element_type=F32)
                    o_i = o_i + jnp.dot(a_off.astype(BF16), vb, preferred_element_type=F32)
                for s in range(SUB):
                    dec = jnp.exp(jnp.minimum(b_i - b_i[s:s + 1], 0.0))
                    a_s = jnp.sum(jnp.where(row_s >= s, q_i * (k_i[s:s + 1] * dec), 0.0), -1,
                                  keepdims=True)
                    o_i = o_i + a_s * v_i[s:s + 1]
                outs.append(o_i)
            st_s[...] = st_new
            finish(r0, jnp.concatenate(outs, axis=0))
            return carry

        lax.fori_loop(0, nchunk, chunk_step, 0)


def _hgrn(proj, lb_logits, norm_w, layer, batch, seq, tb=512):
    nblk = seq // tb
    w = G_DK
    col = lambda off: (lambda b, h, s: (b * nblk + s, off // w + h))
    vec = pl.BlockSpec((1, w), lambda b, h, s: (0, h))
    return pl.pallas_call(
        functools.partial(_hgrn_kernel, layer=layer),
        out_shape=jax.ShapeDtypeStruct((batch * seq, G_WV), BF16),
        grid=(batch, G_HEADS, nblk),
        in_specs=[pl.BlockSpec((DEPTH, w), lambda b, h, s: (0, h)),
                  pl.BlockSpec((tb, w), col(OFF_GQ)),
                  pl.BlockSpec((tb, w), col(OFF_GF)),
                  pl.BlockSpec((tb, w), col(OFF_GI)),
                  pl.BlockSpec((tb, w), col(OFF_GG)),
                  vec],
        out_specs=pl.BlockSpec((tb, w), lambda b, h, s: (b * nblk + s, h)),
        scratch_shapes=[pltpu.VMEM((tb, w), F32), pltpu.VMEM((tb, w), F32), pltpu.VMEM((tb, w), F32),
                        pltpu.VMEM((w, w), F32)],
        compiler_params=_params("parallel", "parallel", "arbitrary"),
        name="hgrn2",
    )(lb_logits, proj, proj, proj, proj, norm_w.reshape(1, -1))


def _mix_kernel(hm_ref, ha_ref, hg_ref, g0_ref, g1_ref, g2_ref, pm_ref, pa_ref, pg_ref, wo_ref,
                h_ref, lw_ref, lb_ref, o_ref, obf_ref):
    def branch(x_ref, p_ref, g_ref):
        return (_gate_sigmoid(g_ref[...].astype(F32))
                * jnp.dot(x_ref[...], p_ref[...], preferred_element_type=F32))

    y = branch(hm_ref, pm_ref, g0_ref) + branch(ha_ref, pa_ref, g1_ref) + branch(hg_ref, pg_ref, g2_ref)
    mix = jnp.dot(y.astype(BF16), wo_ref[...], preferred_element_type=F32)
    h = _layer_norm(ALPHA * h_ref[...] + mix, lw_ref[...], lb_ref[...])
    o_ref[...] = h
    obf_ref[...] = h.astype(BF16)


def _mix_ln(hm, ha, hg, gate_pre, p_m, p_a, p_g, w_out, layer, h, ln_w, ln_b, tm=256):
    m, d = h.shape
    row = pl.BlockSpec((tm, d), lambda i: (i, 0))
    vec = pl.BlockSpec((1, d), lambda i: (0, 0))
    act = lambda w: pl.BlockSpec((tm, w), lambda i: (i, 0))
    gate = lambda br: pl.BlockSpec((tm, d), lambda i: (i, br))
    wgt = lambda w: pl.BlockSpec((None,) + w.shape[1:], lambda i: (layer, 0, 0),
                                 pipeline_mode=pl.Buffered(1))
    return pl.pallas_call(
        _mix_kernel,
        out_shape=(jax.ShapeDtypeStruct((m, d), F32), jax.ShapeDtypeStruct((m, d), BF16)),
        grid=(m // tm,),
        in_specs=[act(M_W), act(A_W), act(G_WV), gate(0), gate(1), gate(2),
                  wgt(p_m), wgt(p_a), wgt(p_g), wgt(w_out), row, vec, vec],
        out_specs=(row, row),
        compiler_params=_params("parallel"),
        name="mix_ln",
    )(hm, ha, hg, gate_pre, gate_pre, gate_pre, p_m, p_a, p_g, w_out, h,
      ln_w.reshape(1, d), ln_b.reshape(1, d))


def _mlp_kernel(hbf_ref, wu_ref, wd_ref, h_ref, lw_ref, lb_ref, o_ref, obf_ref, acc):
    j = pl.program_id(1)

    @pl.when(j == 0)
    def _():
        acc[...] = jnp.zeros_like(acc)

    up = jnp.maximum(jnp.dot(hbf_ref[...], wu_ref[...], preferred_element_type=F32), 0.0)
    acc[...] += jnp.dot((up * up).astype(BF16), wd_ref[...], preferred_element_type=F32)

    @pl.when(j == pl.num_programs(1) - 1)
    def _():
        h = _layer_norm(ALPHA * h_ref[...] + acc[...], lw_ref[...], lb_ref[...])
        o_ref[...] = h
        obf_ref[...] = h.astype(BF16)


def _mlp_ln(hbf, w_up, w_down, layer, h, ln_w, ln_b, tm=512, tf=512):
    m, d = h.shape
    f = w_up.shape[2]
    row = pl.BlockSpec((tm, d), lambda i, j: (i, 0))
    vec = pl.BlockSpec((1, d), lambda i, j: (0, 0))
    return pl.pallas_call(
        _mlp_kernel,
        out_shape=(jax.ShapeDtypeStruct((m, d), F32), jax.ShapeDtypeStruct((m, d), BF16)),
        grid=(m // tm, f // tf),
        in_specs=[row, pl.BlockSpec((None, d, tf), lambda i, j: (layer, 0, j)),
                  pl.BlockSpec((None, tf, d), lambda i, j: (layer, j, 0)), row, vec, vec],
        out_specs=(row, row),
        scratch_shapes=[pltpu.VMEM((tm, d), F32)],
        compiler_params=_params("parallel", "arbitrary"),
        name="mlp_ln",
    )(hbf, w_up, w_down, h, ln_w.reshape(1, d), ln_b.reshape(1, d))


def kernel(x, positions, ln0_w, ln0_b, w_in, m_conv_w, m_conv_b, m_gate_b, m_norm_w, a_lambda, a_norm_w,
           g_lb_logits, g_norm_w, p_m, p_a, p_g, w_out, ln1_w, ln1_b, w_up, w_down, ln2_w, ln2_b):
    batch, seq, d = x.shape
    m = batch * seq
    pos = positions.reshape(m, 1)
    inv = jnp.power(jnp.float32(ROPE_THETA), -jnp.arange(0, ROT_DIM, 2, dtype=F32) / ROT_DIM)
    inv = jnp.tile(inv, LANES // inv.shape[0]).reshape(1, LANES)

    w_mg = w_in[:, :, :PROJ_M_W + LANES].astype(BF16)
    w_r = w_in[:, :, PROJ_M_W + 2 * M_HEADS:].astype(BF16)
    p_m, p_a, p_g, w_out, w_up, w_down = (t.astype(BF16) for t in (p_m, p_a, p_g, w_out, w_up, w_down))
    gate_b = jnp.pad(m_gate_b, ((0, 0), (0, LANES - 2 * M_HEADS)))

    h, hbf = _ln(x.reshape(m, d), ln0_w, ln0_b)
    for l in range(DEPTH):
        proj_m = _matmul(hbf, w_mg, l, 0, PROJ_M_W, BF16, 1024, 512, "in_proj_m")
        proj_r = _matmul(hbf, w_r, l, 0, OFF_GATE, BF16, 1024, 512, "in_proj_r")
        proj_g = _matmul(hbf, w_r, l, OFF_GATE, N_BRANCH * D_MODEL, BF16, 1024, 512, "in_proj_g")
        gates = _matmul(hbf, w_mg, l, PROJ_M_W, LANES, F32, 1024, LANES, "gate_proj")

        hm = _mlstm(proj_m, gates, gate_b[l:l + 1], m_conv_w[l], m_conv_b[l], m_norm_w[l], batch, seq)
        aq, ak = _rope(proj_r, pos, inv)
        lam_init = 0.8 - 0.6 * math.exp(-0.3 * l)
        ha = _attention(aq, ak, proj_r, a_lambda[l], a_norm_w[l], lam_init, batch, seq)
        hg = _hgrn(proj_r, g_lb_logits, g_norm_w[l], l, batch, seq)

        h, hbf = _mix_ln(hm, ha, hg, proj_g, p_m, p_a, p_g, w_out, l, h, ln1_w[l], ln1_b[l])
        h, hbf = _mlp_ln(hbf, w_up, w_down, l, h, ln2_w[l], ln2_b[l])
    return h.reshape(batch, seq, d)
```

```python
import functools
import math

import jax
import jax.numpy as jnp
from jax import lax
from jax.experimental import pallas as pl
from jax.experimental.pallas import tpu as pltpu

F32 = jnp.float32
BF16 = jnp.bfloat16

D_MODEL = 2048
DEPTH = 2
M_HEADS = 4
M_DH = 256
M_W = M_HEADS * M_DH
CONV_K = 4
A_HEADS = 8
A_DH = 64
A_W = A_HEADS * 2 * A_DH
ROT_DIM = A_DH // 4
ROPE_THETA = 500000.0
G_HEADS = 8
G_DK = 128
G_WK = G_HEADS * G_DK
G_WV = G_HEADS * G_DK
CHUNK = 64
M_CHUNK = 256
SUB = 16
FACTOR_SPREAD = 60.0
N_BRANCH = 3
D_FF = 4 * D_MODEL
ALPHA = (2 * DEPTH) ** 0.25
LN_EPS = 1e-5
NORM_EPS = 1e-6
NEG_BIG = -1e30
F_FLOOR = 1e-30

OFF_MQ, OFF_MK, OFF_MV, OFF_MO = 0, M_W, 2 * M_W, 3 * M_W
PROJ_M_W = 4 * M_W
OFF_AQ = 0
OFF_AK = OFF_AQ + A_W
OFF_AV = OFF_AK + A_W
OFF_GQ = OFF_AV + A_W
OFF_GF = OFF_GQ + G_WK
OFF_GI = OFF_GF + G_WK
OFF_GG = OFF_GI + G_WV
OFF_GATE = OFF_GG + G_WV
PROJ_R_W = OFF_GATE + N_BRANCH * D_MODEL
LANES = 128
VMEM_LIMIT = 48 * 1024 * 1024

NT_DIMS = (((1,), (1,)), ((), ()))
TN_DIMS = (((0,), (0,)), ((), ()))


def _params(*sem):
    return pltpu.CompilerParams(dimension_semantics=sem, vmem_limit_bytes=VMEM_LIMIT)


def _sigmoid(x):
    return 1.0 / (1.0 + jnp.exp(-x))


def _gate_sigmoid(x):
    return 0.5 * jnp.tanh(0.5 * x) + 0.5


def _layer_norm(x, w, b):
    mu = jnp.mean(x, -1, keepdims=True)
    xc = x - mu
    var = jnp.mean(xc * xc, -1, keepdims=True)
    return xc * lax.rsqrt(var + LN_EPS) * w + b


def _rms_norm(x, w):
    return x * lax.rsqrt(jnp.mean(x * x, -1, keepdims=True) + NORM_EPS) * w


def _chunk_cumsum(x, chunk):
    row = lax.broadcasted_iota(jnp.int32, x.shape, 0) & (chunk - 1)
    sh = 1
    while sh < chunk:
        x = x + jnp.where(row >= sh, pltpu.roll(x, sh, axis=0), 0.0)
        sh *= 2
    return x


def _ln_kernel(x_ref, w_ref, b_ref, o_ref, obf_ref):
    y = _layer_norm(x_ref[...], w_ref[...], b_ref[...])
    o_ref[...] = y
    obf_ref[...] = y.astype(BF16)


def _ln(x, w, b, tm=256):
    m, d = x.shape
    return pl.pallas_call(
        _ln_kernel,
        out_shape=(jax.ShapeDtypeStruct((m, d), F32), jax.ShapeDtypeStruct((m, d), BF16)),
        grid=(m // tm,),
        in_specs=[pl.BlockSpec((tm, d), lambda i: (i, 0)),
                  pl.BlockSpec((1, d), lambda i: (0, 0)),
                  pl.BlockSpec((1, d), lambda i: (0, 0))],
        out_specs=(pl.BlockSpec((tm, d), lambda i: (i, 0)),
                   pl.BlockSpec((tm, d), lambda i: (i, 0))),
        compiler_params=_params("parallel"),
        name="ln0",
    )(x, w.reshape(1, d), b.reshape(1, d))


def _mm_kernel(a_ref, b_ref, o_ref):
    o_ref[...] = jnp.dot(a_ref[...], b_ref[...], preferred_element_type=F32).astype(o_ref.dtype)


def _matmul(a, w, layer, col0, n, out_dtype, tm, tn, name):
    m, k = a.shape
    return pl.pallas_call(
        _mm_kernel,
        out_shape=jax.ShapeDtypeStruct((m, n), out_dtype),
        grid=(m // tm, n // tn),
        in_specs=[pl.BlockSpec((tm, k), lambda i, j: (i, 0)),
                  pl.BlockSpec((None, k, tn), lambda i, j: (layer, 0, col0 // tn + j))],
        out_specs=pl.BlockSpec((tm, tn), lambda i, j: (i, j)),
        compiler_params=_params("parallel", "arbitrary"),
        name=name,
    )(a, w)


def _wcast_kernel(*refs, shift):
    o_ref = refs[-1]
    if shift == 0:
        o_ref[...] = refs[0][...].astype(o_ref.dtype)
    else:
        z = jnp.concatenate([refs[0][...], refs[1][...]], axis=1)
        o_ref[...] = z[:, shift:shift + o_ref.shape[1]].astype(o_ref.dtype)


def _wcast(w, col0, n, tr, tn):
    depth, k, _ = w.shape
    shift = col0 % LANES
    base = col0 - shift
    in_specs = [pl.BlockSpec((None, tr, tn), lambda l, r, j: (l, r, base // tn + j))]
    if shift:
        in_specs.append(pl.BlockSpec((None, tr, LANES),
                                     lambda l, r, j: (l, r, (base + (j + 1) * tn) // LANES)))
    return pl.pallas_call(
        functools.partial(_wcast_kernel, shift=shift),
        out_shape=jax.ShapeDtypeStruct((depth, k, n), BF16),
        grid=(depth, k // tr, n // tn),
        in_specs=in_specs,
        out_specs=pl.BlockSpec((None, tr, tn), lambda l, r, j: (l, r, j)),
        compiler_params=_params("parallel", "parallel", "parallel"),
        name="wcast",
    )(*([w] * len(in_specs)))


def _mlstm_kernel(q_ref, k_ref, v_ref, og_ref, g_ref, gb_ref, cw_ref, cb_ref, nw_ref, out_ref,
                  qpad, kpad, c_s, n_s, m_s):
    blk = pl.program_id(1)
    tb = q_ref.shape[0]

    @pl.when(blk == 0)
    def _():
        qpad[0:8, :] = jnp.zeros((8, M_W), F32)
        kpad[0:8, :] = jnp.zeros((8, M_W), F32)
        c_s[...] = jnp.zeros_like(c_s)
        n_s[...] = jnp.zeros_like(n_s)
        m_s[...] = jnp.zeros_like(m_s)

    @pl.when(blk > 0)
    def _():
        qpad[0:8, :] = qpad[tb:tb + 8, :]
        kpad[0:8, :] = kpad[tb:tb + 8, :]

    qpad[8:8 + tb, :] = q_ref[...].astype(F32)
    kpad[8:8 + tb, :] = k_ref[...].astype(F32)

    def conv_silu(pad, cols, wofs):
        wcols = slice(wofs + cols.start, wofs + cols.stop)
        y = cb_ref[:, wcols]
        for j in range(CONV_K):
            off = 8 - (CONV_K - 1) + j
            y = y + cw_ref[j:j + 1, wcols] * pad[off:off + tb, cols]
        return y * _gate_sigmoid(y)

    ri = lax.broadcasted_iota(jnp.int32, (tb, tb), 0)
    ci = lax.broadcasted_iota(jnp.int32, (tb, tb), 1)
    causal = ri >= ci

    gb = g_ref[...] + gb_ref[...]
    log_f = jnp.minimum(gb, 0.0) - jnp.log(1.0 + jnp.exp(-jnp.abs(gb)))
    bcum = _chunk_cumsum(log_f, tb)
    lane = lax.broadcasted_iota(jnp.int32, gb.shape, 1)
    pc = jnp.where(lane < M_HEADS, gb, bcum)
    pt = pc.T

    for h in range(M_HEADS):
        cols = slice(h * M_DH, (h + 1) * M_DH)
        qc = conv_silu(qpad, cols, 0)
        kc = conv_silu(kpad, cols, M_W) * (M_DH ** -0.5)
        vc = v_ref[:, cols]
        qb = qc.astype(BF16)

        i_col, b_col = pc[:, h:h + 1], pc[:, M_HEADS + h:M_HEADS + h + 1]
        i_row, b_row = pt[h:h + 1, :], pt[M_HEADS + h:M_HEADS + h + 1, :]
        m_prev = m_s[h]
        c_prev = c_s[h]
        n_prev = n_s[h]

        log_d = jnp.where(causal, b_col + (i_row - b_row), NEG_BIG)
        log_inter = b_col + m_prev
        m_t = jnp.maximum(log_inter, jnp.max(log_d, -1, keepdims=True))
        w_inter = jnp.exp(log_inter - m_t)
        s = lax.dot_general(qb, kc.astype(BF16), NT_DIMS, preferred_element_type=F32)
        s = s * jnp.exp(log_d - m_t)
        num = (jnp.dot(s.astype(BF16), vc, preferred_element_type=F32)
               + w_inter * jnp.dot(qb, c_prev.astype(BF16), preferred_element_type=F32))
        den = jnp.sum(s, -1, keepdims=True) + w_inter * jnp.sum(qc * n_prev, -1, keepdims=True)
        hh = num / jnp.maximum(jnp.abs(den), jnp.exp(-m_t))

        g = b_col[tb - 1:tb, :]
        log_w = g - b_col + i_col
        m_new = jnp.maximum(g + m_prev, jnp.max(log_w, 0, keepdims=True))
        decay = jnp.exp(g + m_prev - m_new)
        kw = kc * jnp.exp(log_w - m_new)
        c_s[h] = decay * c_prev + lax.dot_general(kw.astype(BF16), vc, TN_DIMS,
                                                  preferred_element_type=F32)
        n_s[h] = decay * n_prev + jnp.sum(kw, 0, keepdims=True)
        m_s[h] = m_new

        hn = _rms_norm(hh, nw_ref[:, cols]) * _gate_sigmoid(og_ref[:, cols].astype(F32))
        out_ref[:, cols] = hn.astype(out_ref.dtype)


def _mlstm(proj, gates, gate_b, conv_w, conv_b, norm_w, batch, seq, tb=M_CHUNK):
    nblk = seq // tb
    col = lambda off: pl.BlockSpec((tb, M_W), lambda b, s: (b * nblk + s, off // M_W))
    full = lambda a: pl.BlockSpec(a.shape, lambda b, s: (0, 0))
    conv_b = conv_b.reshape(1, -1)
    norm_w = norm_w.reshape(1, -1)
    return pl.pallas_call(
        _mlstm_kernel,
        out_shape=jax.ShapeDtypeStruct((batch * seq, M_W), BF16),
        grid=(batch, nblk),
        in_specs=[col(OFF_MQ), col(OFF_MK), col(OFF_MV), col(OFF_MO),
                  pl.BlockSpec((tb, LANES), lambda b, s: (b * nblk + s, 0)),
                  full(gate_b), full(conv_w), full(conv_b), full(norm_w)],
        out_specs=pl.BlockSpec((tb, M_W), lambda b, s: (b * nblk + s, 0)),
        scratch_shapes=[pltpu.VMEM((tb + 8, M_W), F32), pltpu.VMEM((tb + 8, M_W), F32),
                        pltpu.VMEM((M_HEADS, M_DH, M_DH), F32), pltpu.VMEM((M_HEADS, 1, M_DH), F32),
                        pltpu.VMEM((M_HEADS, 1, 1), F32)],
        compiler_params=_params("parallel", "arbitrary"),
        name="mlstm",
    )(proj, proj, proj, proj, gates, gate_b, conv_w, conv_b, norm_w)


def _rope_kernel(pos_ref, inv_ref, q_ref, k_ref, qo_ref, ko_ref):
    ang = pos_ref[...].astype(F32) * inv_ref[...]
    lane = lax.broadcasted_iota(jnp.int32, ang.shape, 1) & (A_DH - 1)
    half = ROT_DIM // 2
    cos, sin = jnp.cos(ang), jnp.sin(ang)
    c_same = jnp.where(lane < ROT_DIM, cos, 1.0)
    c_lo = jnp.where((lane >= half) & (lane < ROT_DIM), sin, 0.0)
    c_hi = jnp.where(lane < half, -sin, 0.0)
    for src, dst in ((q_ref, qo_ref), (k_ref, ko_ref)):
        for g in range(A_W // LANES):
            sl = slice(g * LANES, (g + 1) * LANES)
            x = src[:, sl].astype(F32)
            y = (x * c_same + pltpu.roll(x, half, axis=1) * c_lo
                 + pltpu.roll(x, LANES - half, axis=1) * c_hi)
            dst[:, sl] = y.astype(dst.dtype)


def _rope(proj, pos, inv, tm=256):
    m = proj.shape[0]
    blk = lambda off: pl.BlockSpec((tm, A_W), lambda i: (i, off // A_W))
    out = pl.BlockSpec((tm, A_W), lambda i: (i, 0))
    shp = jax.ShapeDtypeStruct((m, A_W), BF16)
    return pl.pallas_call(
        _rope_kernel,
        out_shape=(shp, shp),
        grid=(m // tm,),
        in_specs=[pl.BlockSpec((tm, 1), lambda i: (i, 0)),
                  pl.BlockSpec((1, LANES), lambda i: (0, 0)),
                  blk(OFF_AQ), blk(OFF_AK)],
        out_specs=(out, out),
        compiler_params=_params("parallel"),
        name="rope",
    )(pos, inv, proj, proj)


def _attn_kernel(lam_ref, q_ref, k_ref, v_ref, nw_ref, o_ref, vext, s_buf, m_s, acc_s, *,
                 lam_init, tq, strip):
    qi = pl.program_id(2)
    hw = 2 * A_DH
    lv = lam_ref[...]
    lam = (jnp.exp(jnp.sum(lv[0:1] * lv[1:2], -1, keepdims=True))
           - jnp.exp(jnp.sum(lv[2:3] * lv[3:4], -1, keepdims=True)) + lam_init)

    @pl.when(qi == 0)
    def _():
        vext[:, :hw] = v_ref[...]
        vext[:, hw:] = jnp.ones((vext.shape[0], hw), BF16)

    q = q_ref[...] * (A_DH ** -0.5)
    lane = lax.broadcasted_iota(jnp.int32, q.shape, 1)
    zero = jnp.zeros_like(q)
    qq = jnp.concatenate([jnp.where(lane < A_DH, q, zero), jnp.where(lane >= A_DH, q, zero)], axis=0)
    m_s[...] = jnp.full(m_s.shape, NEG_BIG, F32)
    acc_s[...] = jnp.zeros_like(acc_s)

    strips = [slice(r, r + strip) for r in range(0, 2 * tq, strip)]

    def scores(j, rows):
        r0 = pl.multiple_of(j * tq, tq)
        return lax.dot_general(qq[rows], k_ref[pl.ds(r0, tq), :], NT_DIMS, preferred_element_type=F32)

    def block(j, diagonal):
        r0 = pl.multiple_of(j * tq, tq)
        v = vext[pl.ds(r0, tq), :]
        for rows in strips:
            if diagonal:
                nk = (rows.start & (tq - 1)) + strip
                s = s_buf[rows, :nk]
                row = (lax.broadcasted_iota(jnp.int32, s.shape, 0) + rows.start) & (tq - 1)
                col = lax.broadcasted_iota(jnp.int32, s.shape, 1)
                s = jnp.where(col <= row, s, NEG_BIG)
            else:
                nk = tq
                s = s_buf[rows, :]
                s_buf[rows, :] = scores(j + 1, rows)
            m_prev = m_s[rows, :]
            m_new = jnp.maximum(m_prev, jnp.max(s, -1, keepdims=True))
            p = jnp.exp(s - jnp.concatenate([m_new] * (nk // LANES), axis=1))
            a = jnp.exp(m_prev - m_new)
            acc_s[rows, :] = (jnp.concatenate([a, a], axis=1) * acc_s[rows, :]
                              + jnp.dot(p.astype(BF16), v[:nk], preferred_element_type=F32))
            m_s[rows, :] = m_new

    def off_diagonal(j, carry):
        block(j, False)
        return carry

    for rows in strips:
        s_buf[rows, :] = scores(0, rows)
    lax.fori_loop(0, qi, off_diagonal, 0)
    block(qi, True)

    acc = acc_s[...]
    o = acc[:, :hw] / acc[:, hw:]
    o = o[:tq] - lam * o[tq:]
    o_ref[...] = (_rms_norm(o, nw_ref[...]) * (1.0 - lam_init)).astype(o_ref.dtype)


def _attention(q, k, proj, lam_vecs, norm_w, lam_init, batch, seq, tq=512, strip=128):
    nq = seq // tq
    hw = 2 * A_DH
    return pl.pallas_call(
        functools.partial(_attn_kernel, lam_init=lam_init, tq=tq, strip=min(strip, 2 * tq)),
        out_shape=jax.ShapeDtypeStruct((batch * seq, A_W), BF16),
        grid=(batch, A_HEADS, nq),
        in_specs=[pl.BlockSpec((4, A_DH), lambda b, h, i: (0, 0)),
                  pl.BlockSpec((tq, hw), lambda b, h, i: (b * nq + i, h)),
                  pl.BlockSpec((seq, hw), lambda b, h, i: (b, h)),
                  pl.BlockSpec((seq, hw), lambda b, h, i: (b, OFF_AV // hw + h)),
                  pl.BlockSpec((1, hw), lambda b, h, i: (0, h))],
        out_specs=pl.BlockSpec((tq, hw), lambda b, h, i: (b * nq + i, h)),
        scratch_shapes=[pltpu.VMEM((seq, 2 * hw), BF16), pltpu.VMEM((2 * tq, tq), F32),
                        pltpu.VMEM((2 * tq, LANES), F32), pltpu.VMEM((2 * tq, 2 * hw), F32)],
        compiler_params=_params("parallel", "parallel", "arbitrary"),
        name="diff_attn",
    )(lam_vecs, q, k, proj, norm_w.reshape(1, -1))


def _hgrn_kernel(lbl_ref, q_ref, f_ref, v_ref, og_ref, nw_ref, out_ref, qs, ks, bs, st_s, *, layer):
    blk = pl.program_id(2)
    tb = q_ref.shape[0]
    nchunk = tb // CHUNK
    mid = CHUNK // 2 - 1

    @pl.when(blk == 0)
    def _():
        st_s[...] = jnp.zeros_like(st_s)

    logits = lbl_ref[...]
    e = jnp.exp(logits - jnp.max(logits, 0, keepdims=True))
    p = e / jnp.sum(e, 0, keepdims=True)
    lb = jnp.sum(p[0:layer + 1], 0, keepdims=True) - p[0:1]

    f_g = lb + (1.0 - lb) * _sigmoid(f_ref[...].astype(F32))
    bs[...] = _chunk_cumsum(jnp.log(jnp.maximum(f_g, F_FLOOR)), CHUNK)
    ks[...] = 1.0 - f_g
    qv = q_ref[...].astype(F32)
    qs[...] = qv * _gate_sigmoid(qv)

    def finish(r0, o):
        gg = og_ref[pl.ds(r0, CHUNK), :].astype(F32)
        hn = _rms_norm(o, nw_ref[...]) * (gg * _gate_sigmoid(gg))
        out_ref[pl.ds(r0, CHUNK), :] = hn.astype(out_ref.dtype)

    def state_terms(bc, qc, kc, vb, st):
        b_last = bc[CHUNK - 1:CHUNK]
        o_inter = lax.dot_general((qc * jnp.exp(bc)).astype(BF16), st.astype(BF16), NT_DIMS,
                                  preferred_element_type=F32)
        k_dec = (kc * jnp.exp(b_last - bc)).astype(BF16)
        st_new = st * jnp.exp(b_last) + lax.dot_general(vb, k_dec, TN_DIMS, preferred_element_type=F32)
        return o_inter, st_new

    spread = jnp.zeros((1, G_DK), F32)
    for c in range(nchunk):
        b_mid = bs[c * CHUNK + mid:c * CHUNK + mid + 1, :]
        spread = jnp.maximum(spread, jnp.maximum(bs[c * CHUNK:c * CHUNK + 1, :] - b_mid,
                                                 b_mid - bs[(c + 1) * CHUNK - 1:(c + 1) * CHUNK, :]))
    narrow = jnp.max(spread) <= FACTOR_SPREAD

    @pl.when(narrow)
    def _():
        ri = lax.broadcasted_iota(jnp.int32, (CHUNK, CHUNK), 0)
        ci = lax.broadcasted_iota(jnp.int32, (CHUNK, CHUNK), 1)
        causal = ri >= ci
        st = st_s[...]
        for c in range(nchunk):
            r0 = c * CHUNK
            bc, qc, kc = bs[r0:r0 + CHUNK, :], qs[r0:r0 + CHUNK, :], ks[r0:r0 + CHUNK, :]
            vb = v_ref[r0:r0 + CHUNK, :]
            b_mid = bc[mid:mid + 1]
            q_sc = (qc * jnp.exp(bc - b_mid)).astype(BF16)
            k_sc = (kc * jnp.exp(b_mid - bc)).astype(BF16)
            a = lax.dot_general(q_sc, k_sc, NT_DIMS, preferred_element_type=F32)
            a = jnp.where(causal, a, 0.0).astype(BF16)
            o_inter, st = state_terms(bc, qc, kc, vb, st)
            finish(r0, o_inter + jnp.dot(a, vb, preferred_element_type=F32))
        st_s[...] = st

    @pl.when(jnp.logical_not(narrow))
    def _():
        row_c = lax.broadcasted_iota(jnp.int32, (CHUNK, G_DK), 0)
        row_s = lax.broadcasted_iota(jnp.int32, (SUB, G_DK), 0)

        def chunk_step(c, carry):
            r0 = pl.multiple_of(c * CHUNK, CHUNK)
            bc = bs[pl.ds(r0, CHUNK), :]
            qc = qs[pl.ds(r0, CHUNK), :]
            kc = ks[pl.ds(r0, CHUNK), :]
            vb = v_ref[pl.ds(r0, CHUNK), :]
            vc = vb.astype(F32)
            o_inter, st_new = state_terms(bc, qc, kc, vb, st_s[...])
            outs = []
            for i in range(CHUNK // SUB):
                r = slice(i * SUB, (i + 1) * SUB)
                b_i, q_i, k_i, v_i = bc[r], qc[r], kc[r], vc[r]
                o_i = o_inter[r]
                if i > 0:
                    c_i = b_i[0:1]
                    q_sc = (q_i * jnp.exp(b_i - c_i)).astype(BF16)
                    k_sc = jnp.where(row_c < i * SUB, kc * jnp.exp(jnp.minimum(c_i - bc, 0.0)), 0.0)
                    a_off = lax.dot_general(q_sc, k_sc.astype(BF16), NT_DIMS,
                                            preferred_element_type=F32)
                    o_i = o_i + jnp.dot(a_off.astype(BF16), vb, preferred_element_type=F32)
                for s in range(SUB):
                    dec = jnp.exp(jnp.minimum(b_i - b_i[s:s + 1], 0.0))
                    a_s = jnp.sum(jnp.where(row_s >= s, q_i * (k_i[s:s + 1] * dec), 0.0), -1,
                                  keepdims=True)
                    o_i = o_i + a_s * v_i[s:s + 1]
                outs.append(o_i)
            st_s[...] = st_new
            finish(r0, jnp.concatenate(outs, axis=0))
            return carry

        lax.fori_loop(0, nchunk, chunk_step, 0)


def _hgrn(proj, lb_logits, norm_w, layer, batch, seq, tb=512):
    nblk = seq // tb
    w = G_DK
    col = lambda off: (lambda b, h, s: (b * nblk + s, off // w + h))
    vec = pl.BlockSpec((1, w), lambda b, h, s: (0, h))
    return pl.pallas_call(
        functools.partial(_hgrn_kernel, layer=layer),
        out_shape=jax.ShapeDtypeStruct((batch * seq, G_WV), BF16),
        grid=(batch, G_HEADS, nblk),
        in_specs=[pl.BlockSpec((DEPTH, w), lambda b, h, s: (0, h)),
                  pl.BlockSpec((tb, w), col(OFF_GQ)),
                  pl.BlockSpec((tb, w), col(OFF_GF)),
                  pl.BlockSpec((tb, w), col(OFF_GI)),
                  pl.BlockSpec((tb, w), col(OFF_GG)),
                  vec],
        out_specs=pl.BlockSpec((tb, w), lambda b, h, s: (b * nblk + s, h)),
        scratch_shapes=[pltpu.VMEM((tb, w), F32), pltpu.VMEM((tb, w), F32), pltpu.VMEM((tb, w), F32),
                        pltpu.VMEM((w, w), F32)],
        compiler_params=_params("parallel", "parallel", "arbitrary"),
        name="hgrn2",
    )(lb_logits, proj, proj, proj, proj, norm_w.reshape(1, -1))


def _mix_kernel(hm_ref, ha_ref, hg_ref, g0_ref, g1_ref, g2_ref, pm_ref, pa_ref, pg_ref, wo_ref,
                h_ref, lw_ref, lb_ref, o_ref, obf_ref):
    def branch(x_ref, p_ref, g_ref):
        return (_gate_sigmoid(g_ref[...].astype(F32))
                * jnp.dot(x_ref[...], p_ref[...], preferred_element_type=F32))

    y = branch(hm_ref, pm_ref, g0_ref) + branch(ha_ref, pa_ref, g1_ref) + branch(hg_ref, pg_ref, g2_ref)
    mix = jnp.dot(y.astype(BF16), wo_ref[...], preferred_element_type=F32)
    h = _layer_norm(ALPHA * h_ref[...] + mix, lw_ref[...], lb_ref[...])
    o_ref[...] = h
    obf_ref[...] = h.astype(BF16)


def _mix_ln(hm, ha, hg, gate_pre, p_m, p_a, p_g, w_out, layer, h, ln_w, ln_b, tm=256):
    m, d = h.shape
    row = pl.BlockSpec((tm, d), lambda i: (i, 0))
    vec = pl.BlockSpec((1, d), lambda i: (0, 0))
    act = lambda w: pl.BlockSpec((tm, w), lambda i: (i, 0))
    gate = lambda br: pl.BlockSpec((tm, d), lambda i: (i, br))
    wgt = lambda w: pl.BlockSpec((None,) + w.shape[1:], lambda i: (layer, 0, 0),
                                 pipeline_mode=pl.Buffered(1))
    return pl.pallas_call(
        _mix_kernel,
        out_shape=(jax.ShapeDtypeStruct((m, d), F32), jax.ShapeDtypeStruct((m, d), BF16)),
        grid=(m // tm,),
        in_specs=[act(M_W), act(A_W), act(G_WV), gate(0), gate(1), gate(2),
                  wgt(p_m), wgt(p_a), wgt(p_g), wgt(w_out), row, vec, vec],
        out_specs=(row, row),
        compiler_params=_params("parallel"),
        name="mix_ln",
    )(hm, ha, hg, gate_pre, gate_pre, gate_pre, p_m, p_a, p_g, w_out, h,
      ln_w.reshape(1, d), ln_b.reshape(1, d))


def _mlp_kernel(hbf_ref, wu_ref, wd_ref, h_ref, lw_ref, lb_ref, o_ref, obf_ref):
    j = pl.program_id(1)

    @pl.when(j == 0)
    def _():
        o_ref[...] = jnp.zeros_like(o_ref)

    up = jnp.maximum(jnp.dot(hbf_ref[...], wu_ref[...], preferred_element_type=F32), 0.0)
    o_ref[...] += jnp.dot((up * up).astype(BF16), wd_ref[...], preferred_element_type=F32)

    @pl.when(j == pl.num_programs(1) - 1)
    def _():
        h = _layer_norm(ALPHA * h_ref[...] + o_ref[...], lw_ref[...], lb_ref[...])
        o_ref[...] = h
        obf_ref[...] = h.astype(BF16)


def _mlp_ln(hbf, w_up, w_down, layer, h, ln_w, ln_b, tm=512, tf=1024):
    m, d = h.shape
    f = w_up.shape[2]
    row = pl.BlockSpec((tm, d), lambda i, j: (i, 0))
    vec = pl.BlockSpec((1, d), lambda i, j: (0, 0))
    return pl.pallas_call(
        _mlp_kernel,
        out_shape=(jax.ShapeDtypeStruct((m, d), F32), jax.ShapeDtypeStruct((m, d), BF16)),
        grid=(m // tm, f // tf),
        in_specs=[row, pl.BlockSpec((None, d, tf), lambda i, j: (layer, 0, j)),
                  pl.BlockSpec((None, tf, d), lambda i, j: (layer, j, 0)), row, vec, vec],
        out_specs=(row, row),
        compiler_params=_params("parallel", "arbitrary"),
        name="mlp_ln",
    )(hbf, w_up, w_down, h, ln_w.reshape(1, d), ln_b.reshape(1, d))


def kernel(x, positions, ln0_w, ln0_b, w_in, m_conv_w, m_conv_b, m_gate_b, m_norm_w, a_lambda, a_norm_w,
           g_lb_logits, g_norm_w, p_m, p_a, p_g, w_out, ln1_w, ln1_b, w_up, w_down, ln2_w, ln2_b):
    batch, seq, d = x.shape
    m = batch * seq
    pos = positions.reshape(m, 1)
    inv = jnp.power(jnp.float32(ROPE_THETA), -jnp.arange(0, ROT_DIM, 2, dtype=F32) / ROT_DIM)
    inv = jnp.tile(inv, LANES // inv.shape[0]).reshape(1, LANES)

    w_m = _wcast(w_in, 0, PROJ_M_W, 1024, 1024)
    w_gl = _wcast(w_in, PROJ_M_W, LANES, 1024, LANES)
    w_r = _wcast(w_in, PROJ_M_W + 2 * M_HEADS, PROJ_R_W, 1024, 1024)
    p_m, p_a, p_g, w_out, w_up, w_down = (t.astype(BF16) for t in (p_m, p_a, p_g, w_out, w_up, w_down))
    gate_b = jnp.pad(m_gate_b, ((0, 0), (0, LANES - 2 * M_HEADS)))

    h, hbf = _ln(x.reshape(m, d), ln0_w, ln0_b)
    for l in range(DEPTH):
        proj_m = _matmul(hbf, w_m, l, 0, PROJ_M_W, BF16, 2048, 512, "in_proj_m")
        proj_r = _matmul(hbf, w_r, l, 0, OFF_GATE, BF16, 2048, 512, "in_proj_r")
        proj_g = _matmul(hbf, w_r, l, OFF_GATE, N_BRANCH * D_MODEL, BF16, 2048, 512, "in_proj_g")
        gates = _matmul(hbf, w_gl, l, 0, LANES, F32, 2048, LANES, "gate_proj")

        hm = _mlstm(proj_m, gates, gate_b[l:l + 1], m_conv_w[l], m_conv_b[l], m_norm_w[l], batch, seq)
        aq, ak = _rope(proj_r, pos, inv)
        lam_init = 0.8 - 0.6 * math.exp(-0.3 * l)
        ha = _attention(aq, ak, proj_r, a_lambda[l], a_norm_w[l], lam_init, batch, seq)
        hg = _hgrn(proj_r, g_lb_logits, g_norm_w[l], l, batch, seq)

        h, hbf = _mix_ln(hm, ha, hg, proj_g, p_m, p_a, p_g, w_out, l, h, ln1_w[l], ln1_b[l])
        h, hbf = _mlp_ln(hbf, w_up, w_down, l, h, ln2_w[l], ln2_b[l])
    return h.reshape(batch, seq, d)
```

```python
import functools
import math

import jax
import jax.numpy as jnp
from jax import lax
from jax.experimental import pallas as pl
from jax.experimental.pallas import tpu as pltpu

F32 = jnp.float32
BF16 = jnp.bfloat16

D_MODEL = 2048
DEPTH = 2
M_HEADS = 4
M_DH = 256
M_W = M_HEADS * M_DH
CONV_K = 4
A_HEADS = 8
A_DH = 64
A_W = A_HEADS * 2 * A_DH
ROT_DIM = A_DH // 4
ROPE_THETA = 500000.0
G_HEADS = 8
G_DK = 128
G_WK = G_HEADS * G_DK
G_WV = G_HEADS * G_DK
CHUNK = 64
M_CHUNK = 256
SUB = 16
FACTOR_SPREAD = 60.0
N_BRANCH = 3
D_FF = 4 * D_MODEL
ALPHA = (2 * DEPTH) ** 0.25
LN_EPS = 1e-5
NORM_EPS = 1e-6
NEG_BIG = -1e30
F_FLOOR = 1e-30

OFF_MQ, OFF_MK, OFF_MV, OFF_MO = 0, M_W, 2 * M_W, 3 * M_W
PROJ_M_W = 4 * M_W
OFF_AQ = 0
OFF_AK = OFF_AQ + A_W
OFF_AV = OFF_AK + A_W
OFF_GQ = OFF_AV + A_W
OFF_GF = OFF_GQ + G_WK
OFF_GI = OFF_GF + G_WK
OFF_GG = OFF_GI + G_WV
OFF_GATE = OFF_GG + G_WV
PROJ_R_W = OFF_GATE + N_BRANCH * D_MODEL
LANES = 128
VMEM_LIMIT = 48 * 1024 * 1024

NT_DIMS = (((1,), (1,)), ((), ()))
TN_DIMS = (((0,), (0,)), ((), ()))


def _params(*sem):
    return pltpu.CompilerParams(dimension_semantics=sem, vmem_limit_bytes=VMEM_LIMIT)


def _sigmoid(x):
    return 1.0 / (1.0 + jnp.exp(-x))


def _gate_sigmoid(x):
    return 0.5 * jnp.tanh(0.5 * x) + 0.5


def _layer_norm(x, w, b):
    mu = jnp.mean(x, -1, keepdims=True)
    xc = x - mu
    var = jnp.mean(xc * xc, -1, keepdims=True)
    return xc * lax.rsqrt(var + LN_EPS) * w + b


def _rms_norm(x, w):
    return x * lax.rsqrt(jnp.mean(x * x, -1, keepdims=True) + NORM_EPS) * w


def _chunk_cumsum(x, chunk):
    row = lax.broadcasted_iota(jnp.int32, x.shape, 0) & (chunk - 1)
    sh = 1
    while sh < chunk:
        x = x + jnp.where(row >= sh, pltpu.roll(x, sh, axis=0), 0.0)
        sh *= 2
    return x


def _ln_kernel(x_ref, w_ref, b_ref, o_ref, obf_ref):
    y = _layer_norm(x_ref[...], w_ref[...], b_ref[...])
    o_ref[...] = y
    obf_ref[...] = y.astype(BF16)


def _ln(x, w, b, tm=256):
    m, d = x.shape
    return pl.pallas_call(
        _ln_kernel,
        out_shape=(jax.ShapeDtypeStruct((m, d), F32), jax.ShapeDtypeStruct((m, d), BF16)),
        grid=(m // tm,),
        in_specs=[pl.BlockSpec((tm, d), lambda i: (i, 0)),
                  pl.BlockSpec((1, d), lambda i: (0, 0)),
                  pl.BlockSpec((1, d), lambda i: (0, 0))],
        out_specs=(pl.BlockSpec((tm, d), lambda i: (i, 0)),
                   pl.BlockSpec((tm, d), lambda i: (i, 0))),
        compiler_params=_params("parallel"),
        name="ln0",
    )(x, w.reshape(1, d), b.reshape(1, d))


def _proj_kernel(a_ref, w_ref, *rest, shift, pad):
    o_ref = rest[-1]
    w = w_ref[...]
    if shift:
        w = jnp.concatenate([w[shift:], rest[0][...]], axis=0)
    if pad:
        w = jnp.concatenate([w, jnp.zeros((pad, w.shape[1]), w.dtype)], axis=0)
    o_ref[...] = lax.dot_general(a_ref[...], w.astype(BF16), NT_DIMS,
                                 preferred_element_type=F32).astype(o_ref.dtype)


def _in_proj(a, w_t, layer, row0, n, out_dtype, tm, tn, name):
    m, k = a.shape
    shift = row0 % tn
    base = row0 - shift
    rows = min(tn, n)
    pad = tn - rows
    in_specs = [pl.BlockSpec((tm, k), lambda i, j: (i, 0)),
                pl.BlockSpec((None, rows, k), lambda i, j: (layer, base // rows + j, 0))]
    if shift:
        in_specs.append(pl.BlockSpec((None, shift, k),
                                     lambda i, j: (layer, (base + (j + 1) * tn) // shift, 0)))
    return pl.pallas_call(
        functools.partial(_proj_kernel, shift=shift, pad=pad),
        out_shape=jax.ShapeDtypeStruct((m, max(n, tn)), out_dtype),
        grid=(m // tm, max(n // tn, 1)),
        in_specs=in_specs,
        out_specs=pl.BlockSpec((tm, tn), lambda i, j: (i, j)),
        compiler_params=_params("parallel", "arbitrary"),
        name=name,
    )(*([a] + [w_t] * (len(in_specs) - 1)))


def _mlstm_kernel(q_ref, k_ref, v_ref, og_ref, g_ref, gb_ref, cw_ref, cb_ref, nw_ref, out_ref,
                  qpad, kpad, c_s, n_s, m_s):
    blk = pl.program_id(1)
    tb = q_ref.shape[0]

    @pl.when(blk == 0)
    def _():
        qpad[0:8, :] = jnp.zeros((8, M_W), F32)
        kpad[0:8, :] = jnp.zeros((8, M_W), F32)
        c_s[...] = jnp.zeros_like(c_s)
        n_s[...] = jnp.zeros_like(n_s)
        m_s[...] = jnp.zeros_like(m_s)

    @pl.when(blk > 0)
    def _():
        qpad[0:8, :] = qpad[tb:tb + 8, :]
        kpad[0:8, :] = kpad[tb:tb + 8, :]

    qpad[8:8 + tb, :] = q_ref[...].astype(F32)
    kpad[8:8 + tb, :] = k_ref[...].astype(F32)

    def conv_silu(pad, cols, wofs):
        wcols = slice(wofs + cols.start, wofs + cols.stop)
        y = cb_ref[:, wcols]
        for j in range(CONV_K):
            off = 8 - (CONV_K - 1) + j
            y = y + cw_ref[j:j + 1, wcols] * pad[off:off + tb, cols]
        return y * _gate_sigmoid(y)

    ri = lax.broadcasted_iota(jnp.int32, (tb, tb), 0)
    ci = lax.broadcasted_iota(jnp.int32, (tb, tb), 1)
    causal = ri >= ci

    gb = g_ref[...] + gb_ref[...]
    log_f = jnp.minimum(gb, 0.0) - jnp.log(1.0 + jnp.exp(-jnp.abs(gb)))
    bcum = _chunk_cumsum(log_f, tb)
    lane = lax.broadcasted_iota(jnp.int32, gb.shape, 1)
    pc = jnp.where(lane < M_HEADS, gb, bcum)
    pt = pc.T

    for h in range(M_HEADS):
        cols = slice(h * M_DH, (h + 1) * M_DH)
        qc = conv_silu(qpad, cols, 0)
        kc = conv_silu(kpad, cols, M_W) * (M_DH ** -0.5)
        vc = v_ref[:, cols]
        qb = qc.astype(BF16)

        i_col, b_col = pc[:, h:h + 1], pc[:, M_HEADS + h:M_HEADS + h + 1]
        i_row, b_row = pt[h:h + 1, :], pt[M_HEADS + h:M_HEADS + h + 1, :]
        m_prev = m_s[h]
        c_prev = c_s[h]
        n_prev = n_s[h]

        log_d = jnp.where(causal, b_col + (i_row - b_row), NEG_BIG)
        log_inter = b_col + m_prev
        m_t = jnp.maximum(log_inter, jnp.max(log_d, -1, keepdims=True))
        w_inter = jnp.exp(log_inter - m_t)
        s = lax.dot_general(qb, kc.astype(BF16), NT_DIMS, preferred_element_type=F32)
        s = s * jnp.exp(log_d - m_t)
        num = (jnp.dot(s.astype(BF16), vc, preferred_element_type=F32)
               + w_inter * jnp.dot(qb, c_prev.astype(BF16), preferred_element_type=F32))
        den = jnp.sum(s, -1, keepdims=True) + w_inter * jnp.sum(qc * n_prev, -1, keepdims=True)
        hh = num / jnp.maximum(jnp.abs(den), jnp.exp(-m_t))

        g = b_col[tb - 1:tb, :]
        log_w = g - b_col + i_col
        m_new = jnp.maximum(g + m_prev, jnp.max(log_w, 0, keepdims=True))
        decay = jnp.exp(g + m_prev - m_new)
        kw = kc * jnp.exp(log_w - m_new)
        c_s[h] = decay * c_prev + lax.dot_general(kw.astype(BF16), vc, TN_DIMS,
                                                  preferred_element_type=F32)
        n_s[h] = decay * n_prev + jnp.sum(kw, 0, keepdims=True)
        m_s[h] = m_new

        hn = _rms_norm(hh, nw_ref[:, cols]) * _gate_sigmoid(og_ref[:, cols].astype(F32))
        out_ref[:, cols] = hn.astype(out_ref.dtype)


def _mlstm(proj, gates, gate_b, conv_w, conv_b, norm_w, batch, seq, tb=M_CHUNK):
    nblk = seq // tb
    col = lambda off: pl.BlockSpec((tb, M_W), lambda b, s: (b * nblk + s, off // M_W))
    full = lambda a: pl.BlockSpec(a.shape, lambda b, s: (0, 0))
    conv_b = conv_b.reshape(1, -1)
    norm_w = norm_w.reshape(1, -1)
    return pl.pallas_call(
        _mlstm_kernel,
        out_shape=jax.ShapeDtypeStruct((batch * seq, M_W), BF16),
        grid=(batch, nblk),
        in_specs=[col(OFF_MQ), col(OFF_MK), col(OFF_MV), col(OFF_MO),
                  pl.BlockSpec((tb, LANES), lambda b, s: (b * nblk + s, 0)),
                  full(gate_b), full(conv_w), full(conv_b), full(norm_w)],
        out_specs=pl.BlockSpec((tb, M_W), lambda b, s: (b * nblk + s, 0)),
        scratch_shapes=[pltpu.VMEM((tb + 8, M_W), F32), pltpu.VMEM((tb + 8, M_W), F32),
                        pltpu.VMEM((M_HEADS, M_DH, M_DH), F32), pltpu.VMEM((M_HEADS, 1, M_DH), F32),
                        pltpu.VMEM((M_HEADS, 1, 1), F32)],
        compiler_params=_params("parallel", "arbitrary"),
        name="mlstm",
    )(proj, proj, proj, proj, gates, gate_b, conv_w, conv_b, norm_w)


def _rope_table_kernel(pos_ref, inv_ref, same_ref, lo_ref, hi_ref):
    ang = pos_ref[...].astype(F32) * inv_ref[...]
    lane = lax.broadcasted_iota(jnp.int32, ang.shape, 1) & (A_DH - 1)
    half = ROT_DIM // 2
    cos, sin = jnp.cos(ang), jnp.sin(ang)
    same_ref[...] = jnp.where(lane < ROT_DIM, cos, 1.0)
    lo_ref[...] = jnp.where((lane >= half) & (lane < ROT_DIM), sin, 0.0)
    hi_ref[...] = jnp.where(lane < half, -sin, 0.0)


def _rope_tables(pos, inv, tm=512):
    m = pos.shape[0]
    out = pl.BlockSpec((tm, LANES), lambda i: (i, 0))
    shp = jax.ShapeDtypeStruct((m, LANES), F32)
    return pl.pallas_call(
        _rope_table_kernel,
        out_shape=(shp, shp, shp),
        grid=(m // tm,),
        in_specs=[pl.BlockSpec((tm, 1), lambda i: (i, 0)), pl.BlockSpec((1, LANES), lambda i: (0, 0))],
        out_specs=(out, out, out),
        compiler_params=_params("parallel"),
        name="rope_tables",
    )(pos, inv)


def _rotate(x, same, lo, hi):
    half = ROT_DIM // 2
    x = x.astype(F32)
    return x * same + pltpu.roll(x, half, axis=1) * lo + pltpu.roll(x, LANES - half, axis=1) * hi


def _attn_kernel(lam_ref, q_ref, k_ref, v_ref, same_ref, lo_ref, hi_ref, nw_ref, o_ref,
                 krot, vext, s_buf, m_s, acc_s, *, lam_init, tq, strip):
    qi = pl.program_id(2)
    hw = 2 * A_DH
    lv = lam_ref[...]
    lam = (jnp.exp(jnp.sum(lv[0:1] * lv[1:2], -1, keepdims=True))
           - jnp.exp(jnp.sum(lv[2:3] * lv[3:4], -1, keepdims=True)) + lam_init)

    @pl.when(qi == 0)
    def _():
        krot[...] = _rotate(k_ref[...], same_ref[...], lo_ref[...], hi_ref[...]).astype(BF16)
        vext[:, :hw] = v_ref[...]
        vext[:, hw:] = jnp.ones((vext.shape[0], hw), BF16)

    q0 = pl.multiple_of(qi * tq, tq)
    q = _rotate(q_ref[...], same_ref[pl.ds(q0, tq), :], lo_ref[pl.ds(q0, tq), :], hi_ref[pl.ds(q0, tq), :])
    q = (q * (A_DH ** -0.5)).astype(BF16)
    lane = lax.broadcasted_iota(jnp.int32, q.shape, 1)
    zero = jnp.zeros_like(q)
    qq = jnp.concatenate([jnp.where(lane < A_DH, q, zero), jnp.where(lane >= A_DH, q, zero)], axis=0)
    m_s[...] = jnp.full(m_s.shape, NEG_BIG, F32)
    acc_s[...] = jnp.zeros_like(acc_s)

    strips = [slice(r, r + strip) for r in range(0, 2 * tq, strip)]

    def scores(j, rows):
        r0 = pl.multiple_of(j * tq, tq)
        return lax.dot_general(qq[rows], krot[pl.ds(r0, tq), :], NT_DIMS, preferred_element_type=F32)

    def block(j, diagonal):
        r0 = pl.multiple_of(j * tq, tq)
        v = vext[pl.ds(r0, tq), :]
        for rows in strips:
            if diagonal:
                nk = (rows.start & (tq - 1)) + strip
                s = s_buf[rows, :nk]
                row = (lax.broadcasted_iota(jnp.int32, s.shape, 0) + rows.start) & (tq - 1)
                col = lax.broadcasted_iota(jnp.int32, s.shape, 1)
                s = jnp.where(col <= row, s, NEG_BIG)
            else:
                nk = tq
                s = s_buf[rows, :]
                s_buf[rows, :] = scores(j + 1, rows)
            m_prev = m_s[rows, :]
            m_new = jnp.maximum(m_prev, jnp.max(s, -1, keepdims=True))
            p = jnp.exp(s - jnp.concatenate([m_new] * (nk // LANES), axis=1))
            a = jnp.exp(m_prev - m_new)
            acc_s[rows, :] = (jnp.concatenate([a, a], axis=1) * acc_s[rows, :]
                              + jnp.dot(p.astype(BF16), v[:nk], preferred_element_type=F32))
            m_s[rows, :] = m_new

    def off_diagonal(j, carry):
        block(j, False)
        return carry

    for rows in strips:
        s_buf[rows, :] = scores(0, rows)
    lax.fori_loop(0, qi, off_diagonal, 0)
    block(qi, True)

    acc = acc_s[...]
    o = acc[:, :hw] / acc[:, hw:]
    o = o[:tq] - lam * o[tq:]
    o_ref[...] = (_rms_norm(o, nw_ref[...]) * (1.0 - lam_init)).astype(o_ref.dtype)


def _attention(proj, tables, lam_vecs, norm_w, lam_init, batch, seq, tq=512, strip=128):
    nq = seq // tq
    hw = 2 * A_DH
    kv = lambda off: pl.BlockSpec((seq, hw), lambda b, h, i: (b, off // hw + h))
    tab = pl.BlockSpec((seq, LANES), lambda b, h, i: (b, 0))
    return pl.pallas_call(
        functools.partial(_attn_kernel, lam_init=lam_init, tq=tq, strip=min(strip, 2 * tq)),
        out_shape=jax.ShapeDtypeStruct((batch * seq, A_W), BF16),
        grid=(batch, A_HEADS, nq),
        in_specs=[pl.BlockSpec((4, A_DH), lambda b, h, i: (0, 0)),
                  pl.BlockSpec((tq, hw), lambda b, h, i: (b * nq + i, OFF_AQ // hw + h)),
                  kv(OFF_AK), kv(OFF_AV), tab, tab, tab,
                  pl.BlockSpec((1, hw), lambda b, h, i: (0, h))],
        out_specs=pl.BlockSpec((tq, hw), lambda b, h, i: (b * nq + i, h)),
        scratch_shapes=[pltpu.VMEM((seq, hw), BF16), pltpu.VMEM((seq, 2 * hw), BF16),
                        pltpu.VMEM((2 * tq, tq), F32), pltpu.VMEM((2 * tq, LANES), F32),
                        pltpu.VMEM((2 * tq, 2 * hw), F32)],
        compiler_params=_params("parallel", "parallel", "arbitrary"),
        name="diff_attn",
    )(lam_vecs, proj, proj, proj, *tables, norm_w.reshape(1, -1))


def _hgrn_kernel(lbl_ref, q_ref, f_ref, v_ref, og_ref, nw_ref, out_ref, qs, ks, bs, st_s, *, layer):
    blk = pl.program_id(2)
    tb = q_ref.shape[0]
    nchunk = tb // CHUNK
    mid = CHUNK // 2 - 1

    @pl.when(blk == 0)
    def _():
        st_s[...] = jnp.zeros_like(st_s)

    logits = lbl_ref[...]
    e = jnp.exp(logits - jnp.max(logits, 0, keepdims=True))
    p = e / jnp.sum(e, 0, keepdims=True)
    lb = jnp.sum(p[0:layer + 1], 0, keepdims=True) - p[0:1]

    f_g = lb + (1.0 - lb) * _sigmoid(f_ref[...].astype(F32))
    bs[...] = _chunk_cumsum(jnp.log(jnp.maximum(f_g, F_FLOOR)), CHUNK)
    ks[...] = 1.0 - f_g
    qv = q_ref[...].astype(F32)
    qs[...] = qv * _gate_sigmoid(qv)

    def finish(r0, o):
        gg = og_ref[pl.ds(r0, CHUNK), :].astype(F32)
        hn = _rms_norm(o, nw_ref[...]) * (gg * _gate_sigmoid(gg))
        out_ref[pl.ds(r0, CHUNK), :] = hn.astype(out_ref.dtype)

    def state_terms(bc, qc, kc, vb, st):
        b_last = bc[CHUNK - 1:CHUNK]
        o_inter = lax.dot_general((qc * jnp.exp(bc)).astype(BF16), st.astype(BF16), NT_DIMS,
                                  preferred_element_type=F32)
        k_dec = (kc * jnp.exp(b_last - bc)).astype(BF16)
        st_new = st * jnp.exp(b_last) + lax.dot_general(vb, k_dec, TN_DIMS, preferred_element_type=F32)
        return o_inter, st_new

    spread = jnp.zeros((1, G_DK), F32)
    for c in range(nchunk):
        b_mid = bs[c * CHUNK + mid:c * CHUNK + mid + 1, :]
        spread = jnp.maximum(spread, jnp.maximum(bs[c * CHUNK:c * CHUNK + 1, :] - b_mid,
                                                 b_mid - bs[(c + 1) * CHUNK - 1:(c + 1) * CHUNK, :]))
    narrow = jnp.max(spread) <= FACTOR_SPREAD

    @pl.when(narrow)
    def _():
        ri = lax.broadcasted_iota(jnp.int32, (CHUNK, CHUNK), 0)
        ci = lax.broadcasted_iota(jnp.int32, (CHUNK, CHUNK), 1)
        causal = ri >= ci
        st = st_s[...]
        for c in range(nchunk):
            r0 = c * CHUNK
            bc, qc, kc = bs[r0:r0 + CHUNK, :], qs[r0:r0 + CHUNK, :], ks[r0:r0 + CHUNK, :]
            vb = v_ref[r0:r0 + CHUNK, :]
            b_mid = bc[mid:mid + 1]
            q_sc = (qc * jnp.exp(bc - b_mid)).astype(BF16)
            k_sc = (kc * jnp.exp(b_mid - bc)).astype(BF16)
            a = lax.dot_general(q_sc, k_sc, NT_DIMS, preferred_element_type=F32)
            a = jnp.where(causal, a, 0.0).astype(BF16)
            o_inter, st = state_terms(bc, qc, kc, vb, st)
            finish(r0, o_inter + jnp.dot(a, vb, preferred_element_type=F32))
        st_s[...] = st

    @pl.when(jnp.logical_not(narrow))
    def _():
        row_c = lax.broadcasted_iota(jnp.int32, (CHUNK, G_DK), 0)
        row_s = lax.broadcasted_iota(jnp.int32, (SUB, G_DK), 0)

        def chunk_step(c, carry):
            r0 = pl.multiple_of(c * CHUNK, CHUNK)
            bc = bs[pl.ds(r0, CHUNK), :]
            qc = qs[pl.ds(r0, CHUNK), :]
            kc = ks[pl.ds(r0, CHUNK), :]
            vb = v_ref[pl.ds(r0, CHUNK), :]
            vc = vb.astype(F32)
            o_inter, st_new = state_terms(bc, qc, kc, vb, st_s[...])
            outs = []
            for i in range(CHUNK // SUB):
                r = slice(i * SUB, (i + 1) * SUB)
                b_i, q_i, k_i, v_i = bc[r], qc[r], kc[r], vc[r]
                o_i = o_inter[r]
                if i > 0:
                    c_i = b_i[0:1]
                    q_sc = (q_i * jnp.exp(b_i - c_i)).astype(BF16)
                    k_sc = jnp.where(row_c < i * SUB, kc * jnp.exp(jnp.minimum(c_i - bc, 0.0)), 0.0)
                    a_off = lax.dot_general(q_sc, k_sc.astype(BF16), NT_DIMS,
                                            preferred_element_type=F32)
                    o_i = o_i + jnp.dot(a_off.astype(BF16), vb, preferred_element_type=F32)
                for s in range(SUB):
                    dec = jnp.exp(jnp.minimum(b_i - b_i[s:s + 1], 0.0))
                    a_s = jnp.sum(jnp.where(row_s >= s, q_i * (k_i[s:s + 1] * dec), 0.0), -1,
                                  keepdims=True)
                    o_i = o_i + a_s * v_i[s:s + 1]
                outs.append(o_i)
            st_s[...] = st_new
            finish(r0, jnp.concatenate(outs, axis=0))
            return carry

        lax.fori_loop(0, nchunk, chunk_step, 0)


def _hgrn(proj, lb_logits, norm_w, layer, batch, seq, tb=1024):
    nblk = seq // tb
    w = G_DK
    col = lambda off: (lambda b, h, s: (b * nblk + s, off // w + h))
    vec = pl.BlockSpec((1, w), lambda b, h, s: (0, h))
    return pl.pallas_call(
        functools.partial(_hgrn_kernel, layer=layer),
        out_shape=jax.ShapeDtypeStruct((batch * seq, G_WV), BF16),
        grid=(batch, G_HEADS, nblk),
        in_specs=[pl.BlockSpec((DEPTH, w), lambda b, h, s: (0, h)),
                  pl.BlockSpec((tb, w), col(OFF_GQ)),
                  pl.BlockSpec((tb, w), col(OFF_GF)),
                  pl.BlockSpec((tb, w), col(OFF_GI)),
                  pl.BlockSpec((tb, w), col(OFF_GG)),
                  vec],
        out_specs=pl.BlockSpec((tb, w), lambda b, h, s: (b * nblk + s, h)),
        scratch_shapes=[pltpu.VMEM((tb, w), F32), pltpu.VMEM((tb, w), F32), pltpu.VMEM((tb, w), F32),
                        pltpu.VMEM((w, w), F32)],
        compiler_params=_params("parallel", "parallel", "arbitrary"),
        name="hgrn2",
    )(lb_logits, proj, proj, proj, proj, norm_w.reshape(1, -1))


def _mix_kernel(hm_ref, ha_ref, hg_ref, g0_ref, g1_ref, g2_ref, pm_ref, pa_ref, pg_ref, wo_ref,
                h_ref, lw_ref, lb_ref, o_ref, obf_ref):
    def branch(x_ref, p_ref, g_ref):
        return (_gate_sigmoid(g_ref[...].astype(F32))
                * jnp.dot(x_ref[...], p_ref[...], preferred_element_type=F32))

    y = branch(hm_ref, pm_ref, g0_ref) + branch(ha_ref, pa_ref, g1_ref) + branch(hg_ref, pg_ref, g2_ref)
    mix = jnp.dot(y.astype(BF16), wo_ref[...], preferred_element_type=F32)
    h = _layer_norm(ALPHA * h_ref[...] + mix, lw_ref[...], lb_ref[...])
    o_ref[...] = h
    obf_ref[...] = h.astype(BF16)


def _mix_ln(hm, ha, hg, gate_pre, p_m, p_a, p_g, w_out, layer, h, ln_w, ln_b, tm=256):
    m, d = h.shape
    row = pl.BlockSpec((tm, d), lambda i: (i, 0))
    vec = pl.BlockSpec((1, d), lambda i: (0, 0))
    act = lambda w: pl.BlockSpec((tm, w), lambda i: (i, 0))
    gate = lambda br: pl.BlockSpec((tm, d), lambda i: (i, br))
    wgt = lambda w: pl.BlockSpec((None,) + w.shape[1:], lambda i: (layer, 0, 0),
                                 pipeline_mode=pl.Buffered(1))
    return pl.pallas_call(
        _mix_kernel,
        out_shape=(jax.ShapeDtypeStruct((m, d), F32), jax.ShapeDtypeStruct((m, d), BF16)),
        grid=(m // tm,),
        in_specs=[act(M_W), act(A_W), act(G_WV), gate(0), gate(1), gate(2),
                  wgt(p_m), wgt(p_a), wgt(p_g), wgt(w_out), row, vec, vec],
        out_specs=(row, row),
        compiler_params=_params("parallel"),
        name="mix_ln",
    )(hm, ha, hg, gate_pre, gate_pre, gate_pre, p_m, p_a, p_g, w_out, h,
      ln_w.reshape(1, d), ln_b.reshape(1, d))


def _mlp_kernel(hbf_ref, wu_ref, wd_ref, h_ref, lw_ref, lb_ref, o_ref, obf_ref):
    j = pl.program_id(1)

    @pl.when(j == 0)
    def _():
        o_ref[...] = jnp.zeros_like(o_ref)

    up = jnp.maximum(jnp.dot(hbf_ref[...], wu_ref[...], preferred_element_type=F32), 0.0)
    o_ref[...] += jnp.dot((up * up).astype(BF16), wd_ref[...], preferred_element_type=F32)

    @pl.when(j == pl.num_programs(1) - 1)
    def _():
        h = _layer_norm(ALPHA * h_ref[...] + o_ref[...], lw_ref[...], lb_ref[...])
        o_ref[...] = h
        obf_ref[...] = h.astype(BF16)


def _mlp_ln(hbf, w_up, w_down, layer, h, ln_w, ln_b, tm=512, tf=1024):
    m, d = h.shape
    f = w_up.shape[2]
    row = pl.BlockSpec((tm, d), lambda i, j: (i, 0))
    vec = pl.BlockSpec((1, d), lambda i, j: (0, 0))
    return pl.pallas_call(
        _mlp_kernel,
        out_shape=(jax.ShapeDtypeStruct((m, d), F32), jax.ShapeDtypeStruct((m, d), BF16)),
        grid=(m // tm, f // tf),
        in_specs=[row, pl.BlockSpec((None, d, tf), lambda i, j: (layer, 0, j)),
                  pl.BlockSpec((None, tf, d), lambda i, j: (layer, j, 0)), row, vec, vec],
        out_specs=(row, row),
        compiler_params=_params("parallel", "arbitrary"),
        name="mlp_ln",
    )(hbf, w_up, w_down, h, ln_w.reshape(1, d), ln_b.reshape(1, d))


def kernel(x, positions, ln0_w, ln0_b, w_in, m_conv_w, m_conv_b, m_gate_b, m_norm_w, a_lambda, a_norm_w,
           g_lb_logits, g_norm_w, p_m, p_a, p_g, w_out, ln1_w, ln1_b, w_up, w_down, ln2_w, ln2_b):
    batch, seq, d = x.shape
    m = batch * seq
    pos = positions.reshape(m, 1)
    inv = jnp.power(jnp.float32(ROPE_THETA), -jnp.arange(0, ROT_DIM, 2, dtype=F32) / ROT_DIM)
    inv = jnp.tile(inv, LANES // inv.shape[0]).reshape(1, LANES)

    w_t = jnp.swapaxes(w_in, 1, 2)
    row_r = PROJ_M_W + 2 * M_HEADS
    p_m, p_a, p_g, w_out, w_up, w_down = (t.astype(BF16) for t in (p_m, p_a, p_g, w_out, w_up, w_down))
    gate_b = jnp.pad(m_gate_b, ((0, 0), (0, LANES - 2 * M_HEADS)))

    tables = _rope_tables(pos, inv)
    h, hbf = _ln(x.reshape(m, d), ln0_w, ln0_b)
    for l in range(DEPTH):
        proj_m = _in_proj(hbf, w_t, l, 0, PROJ_M_W, BF16, 2048, 512, "in_proj_m")
        proj_r = _in_proj(hbf, w_t, l, row_r, OFF_GATE, BF16, 2048, 512, "in_proj_r")
        proj_g = _in_proj(hbf, w_t, l, row_r + OFF_GATE, N_BRANCH * D_MODEL, BF16, 2048, 512, "in_proj_g")
        gates = _in_proj(hbf, w_t, l, PROJ_M_W, 2 * M_HEADS, F32, 2048, LANES, "gate_proj")

        hm = _mlstm(proj_m, gates, gate_b[l:l + 1], m_conv_w[l], m_conv_b[l], m_norm_w[l], batch, seq)
        lam_init = 0.8 - 0.6 * math.exp(-0.3 * l)
        ha = _attention(proj_r, tables, a_lambda[l], a_norm_w[l], lam_init, batch, seq)
        hg = _hgrn(proj_r, g_lb_logits, g_norm_w[l], l, batch, seq)

        h, hbf = _mix_ln(hm, ha, hg, proj_g, p_m, p_a, p_g, w_out, l, h, ln1_w[l], ln1_b[l])
        h, hbf = _mlp_ln(hbf, w_up, w_down, l, h, ln2_w[l], ln2_b[l])
    return h.reshape(batch, seq, d)
```

```python
import functools
import math

import jax
import jax.numpy as jnp
from jax import lax
from jax.experimental import pallas as pl
from jax.experimental.pallas import tpu as pltpu

F32 = jnp.float32
BF16 = jnp.bfloat16

D_MODEL = 2048
DEPTH = 2
M_HEADS = 4
M_DH = 256
M_W = M_HEADS * M_DH
CONV_K = 4
A_HEADS = 8
A_DH = 64
A_W = A_HEADS * 2 * A_DH
ROT_DIM = A_DH // 4
ROPE_THETA = 500000.0
G_HEADS = 8
G_DK = 128
G_WK = G_HEADS * G_DK
G_WV = G_HEADS * G_DK
CHUNK = 64
M_CHUNK = 256
SUB = 16
FACTOR_SPREAD = 60.0
N_BRANCH = 3
D_FF = 4 * D_MODEL
ALPHA = (2 * DEPTH) ** 0.25
LN_EPS = 1e-5
NORM_EPS = 1e-6
NEG_BIG = -1e30
LOG2_E = math.log2(math.e)
F_FLOOR = 1e-30

OFF_MQ, OFF_MK, OFF_MV, OFF_MO = 0, M_W, 2 * M_W, 3 * M_W
PROJ_M_W = 4 * M_W
OFF_AQ = 0
OFF_AK = OFF_AQ + A_W
OFF_AV = OFF_AK + A_W
OFF_GQ = OFF_AV + A_W
OFF_GF = OFF_GQ + G_WK
OFF_GI = OFF_GF + G_WK
OFF_GG = OFF_GI + G_WV
OFF_GATE = OFF_GG + G_WV
PROJ_R_W = OFF_GATE + N_BRANCH * D_MODEL
LANES = 128
VMEM_LIMIT = 48 * 1024 * 1024

NT_DIMS = (((1,), (1,)), ((), ()))
TN_DIMS = (((0,), (0,)), ((), ()))


def _params(*sem):
    return pltpu.CompilerParams(dimension_semantics=sem, vmem_limit_bytes=VMEM_LIMIT)


def _sigmoid(x):
    return 1.0 / (1.0 + jnp.exp(-x))


def _gate_sigmoid(x):
    return 0.5 * jnp.tanh(0.5 * x) + 0.5


def _layer_norm(x, w, b):
    mu = jnp.mean(x, -1, keepdims=True)
    xc = x - mu
    var = jnp.mean(xc * xc, -1, keepdims=True)
    return xc * lax.rsqrt(var + LN_EPS) * w + b


def _rms_norm(x, w):
    return x * lax.rsqrt(jnp.mean(x * x, -1, keepdims=True) + NORM_EPS) * w


def _chunk_cumsum(x, chunk):
    row = lax.broadcasted_iota(jnp.int32, x.shape, 0) & (chunk - 1)
    sh = 1
    while sh < chunk:
        x = x + jnp.where(row >= sh, pltpu.roll(x, sh, axis=0), 0.0)
        sh *= 2
    return x


def _ln_kernel(x_ref, w_ref, b_ref, o_ref, obf_ref):
    y = _layer_norm(x_ref[...], w_ref[...], b_ref[...])
    o_ref[...] = y
    obf_ref[...] = y.astype(BF16)


def _ln(x, w, b, tm=256):
    m, d = x.shape
    return pl.pallas_call(
        _ln_kernel,
        out_shape=(jax.ShapeDtypeStruct((m, d), F32), jax.ShapeDtypeStruct((m, d), BF16)),
        grid=(m // tm,),
        in_specs=[pl.BlockSpec((tm, d), lambda i: (i, 0)),
                  pl.BlockSpec((1, d), lambda i: (0, 0)),
                  pl.BlockSpec((1, d), lambda i: (0, 0))],
        out_specs=(pl.BlockSpec((tm, d), lambda i: (i, 0)),
                   pl.BlockSpec((tm, d), lambda i: (i, 0))),
        compiler_params=_params("parallel"),
        name="ln0",
    )(x, w.reshape(1, d), b.reshape(1, d))


def _proj_kernel(a_ref, w_ref, *rest, shift, pad):
    o_ref = rest[-1]
    w = w_ref[...]
    if shift:
        w = jnp.concatenate([w[shift:], rest[0][...]], axis=0)
    if pad:
        w = jnp.concatenate([w, jnp.zeros((pad, w.shape[1]), w.dtype)], axis=0)
    o_ref[...] = lax.dot_general(a_ref[...], w.astype(BF16), NT_DIMS,
                                 preferred_element_type=F32).astype(o_ref.dtype)


def _in_proj(a, w_t, layer, row0, n, out_dtype, tm, tn, name):
    m, k = a.shape
    shift = row0 % tn
    base = row0 - shift
    rows = min(tn, n)
    pad = tn - rows
    in_specs = [pl.BlockSpec((tm, k), lambda i, j: (i, 0)),
                pl.BlockSpec((None, rows, k), lambda i, j: (layer, base // rows + j, 0))]
    if shift:
        in_specs.append(pl.BlockSpec((None, shift, k),
                                     lambda i, j: (layer, (base + (j + 1) * tn) // shift, 0)))
    return pl.pallas_call(
        functools.partial(_proj_kernel, shift=shift, pad=pad),
        out_shape=jax.ShapeDtypeStruct((m, max(n, tn)), out_dtype),
        grid=(m // tm, max(n // tn, 1)),
        in_specs=in_specs,
        out_specs=pl.BlockSpec((tm, tn), lambda i, j: (i, j)),
        compiler_params=_params("parallel", "arbitrary"),
        name=name,
    )(*([a] + [w_t] * (len(in_specs) - 1)))


def _mlstm_kernel(q_ref, k_ref, v_ref, og_ref, g_ref, gb_ref, cw_ref, cb_ref, nw_ref, out_ref,
                  qpad, kpad, c_s, n_s, m_s):
    blk = pl.program_id(1)
    tb = q_ref.shape[0]

    @pl.when(blk == 0)
    def _():
        qpad[0:8, :] = jnp.zeros((8, M_W), F32)
        kpad[0:8, :] = jnp.zeros((8, M_W), F32)
        c_s[...] = jnp.zeros_like(c_s)
        n_s[...] = jnp.zeros_like(n_s)
        m_s[...] = jnp.zeros_like(m_s)

    @pl.when(blk > 0)
    def _():
        qpad[0:8, :] = qpad[tb:tb + 8, :]
        kpad[0:8, :] = kpad[tb:tb + 8, :]

    qpad[8:8 + tb, :] = q_ref[...].astype(F32)
    kpad[8:8 + tb, :] = k_ref[...].astype(F32)

    def conv_silu(pad, cols, wofs):
        wcols = slice(wofs + cols.start, wofs + cols.stop)
        y = cb_ref[:, wcols]
        for j in range(CONV_K):
            off = 8 - (CONV_K - 1) + j
            y = y + cw_ref[j:j + 1, wcols] * pad[off:off + tb, cols]
        return y * _gate_sigmoid(y)

    ri = lax.broadcasted_iota(jnp.int32, (tb, tb), 0)
    ci = lax.broadcasted_iota(jnp.int32, (tb, tb), 1)
    causal = ri >= ci

    gb = g_ref[...] + gb_ref[...]
    log_f = jnp.minimum(gb, 0.0) - jnp.log(1.0 + jnp.exp(-jnp.abs(gb)))
    bcum = _chunk_cumsum(log_f, tb)
    lane = lax.broadcasted_iota(jnp.int32, gb.shape, 1)
    pc = jnp.where(lane < M_HEADS, gb, bcum)
    pt = pc.T

    for h in range(M_HEADS):
        cols = slice(h * M_DH, (h + 1) * M_DH)
        qc = conv_silu(qpad, cols, 0)
        kc = conv_silu(kpad, cols, M_W) * (M_DH ** -0.5)
        vc = v_ref[:, cols]
        qb = qc.astype(BF16)

        i_col, b_col = pc[:, h:h + 1], pc[:, M_HEADS + h:M_HEADS + h + 1]
        i_row, b_row = pt[h:h + 1, :], pt[M_HEADS + h:M_HEADS + h + 1, :]
        m_prev = m_s[h]
        c_prev = c_s[h]
        n_prev = n_s[h]

        log_d = jnp.where(causal, b_col + (i_row - b_row), NEG_BIG)
        log_inter = b_col + m_prev
        m_t = jnp.maximum(log_inter, jnp.max(log_d, -1, keepdims=True))
        w_inter = jnp.exp(log_inter - m_t)
        s = lax.dot_general(qb, kc.astype(BF16), NT_DIMS, preferred_element_type=F32)
        s = s * jnp.exp(log_d - m_t)
        num = (jnp.dot(s.astype(BF16), vc, preferred_element_type=F32)
               + w_inter * jnp.dot(qb, c_prev.astype(BF16), preferred_element_type=F32))
        den = jnp.sum(s, -1, keepdims=True) + w_inter * jnp.sum(qc * n_prev, -1, keepdims=True)
        hh = num / jnp.maximum(jnp.abs(den), jnp.exp(-m_t))

        g = b_col[tb - 1:tb, :]
        log_w = g - b_col + i_col
        m_new = jnp.maximum(g + m_prev, jnp.max(log_w, 0, keepdims=True))
        decay = jnp.exp(g + m_prev - m_new)
        kw = kc * jnp.exp(log_w - m_new)
        c_s[h] = decay * c_prev + lax.dot_general(kw.astype(BF16), vc, TN_DIMS,
                                                  preferred_element_type=F32)
        n_s[h] = decay * n_prev + jnp.sum(kw, 0, keepdims=True)
        m_s[h] = m_new

        hn = _rms_norm(hh, nw_ref[:, cols]) * _gate_sigmoid(og_ref[:, cols].astype(F32))
        out_ref[:, cols] = hn.astype(out_ref.dtype)


def _mlstm(proj, gates, gate_b, conv_w, conv_b, norm_w, batch, seq, tb=M_CHUNK):
    nblk = seq // tb
    col = lambda off: pl.BlockSpec((tb, M_W), lambda b, s: (b * nblk + s, off // M_W))
    full = lambda a: pl.BlockSpec(a.shape, lambda b, s: (0, 0))
    conv_b = conv_b.reshape(1, -1)
    norm_w = norm_w.reshape(1, -1)
    return pl.pallas_call(
        _mlstm_kernel,
        out_shape=jax.ShapeDtypeStruct((batch * seq, M_W), BF16),
        grid=(batch, nblk),
        in_specs=[col(OFF_MQ), col(OFF_MK), col(OFF_MV), col(OFF_MO),
                  pl.BlockSpec((tb, LANES), lambda b, s: (b * nblk + s, 0)),
                  full(gate_b), full(conv_w), full(conv_b), full(norm_w)],
        out_specs=pl.BlockSpec((tb, M_W), lambda b, s: (b * nblk + s, 0)),
        scratch_shapes=[pltpu.VMEM((tb + 8, M_W), F32), pltpu.VMEM((tb + 8, M_W), F32),
                        pltpu.VMEM((M_HEADS, M_DH, M_DH), F32), pltpu.VMEM((M_HEADS, 1, M_DH), F32),
                        pltpu.VMEM((M_HEADS, 1, 1), F32)],
        compiler_params=_params("parallel", "arbitrary"),
        name="mlstm",
    )(proj, proj, proj, proj, gates, gate_b, conv_w, conv_b, norm_w)


def _rope_table_kernel(pos_ref, inv_ref, same_ref, lo_ref, hi_ref):
    ang = pos_ref[...].astype(F32) * inv_ref[...]
    lane = lax.broadcasted_iota(jnp.int32, ang.shape, 1) & (A_DH - 1)
    half = ROT_DIM // 2
    cos, sin = jnp.cos(ang), jnp.sin(ang)
    same_ref[...] = jnp.where(lane < ROT_DIM, cos, 1.0)
    lo_ref[...] = jnp.where((lane >= half) & (lane < ROT_DIM), sin, 0.0)
    hi_ref[...] = jnp.where(lane < half, -sin, 0.0)


def _rope_tables(pos, inv, tm=512):
    m = pos.shape[0]
    out = pl.BlockSpec((tm, LANES), lambda i: (i, 0))
    shp = jax.ShapeDtypeStruct((m, LANES), F32)
    return pl.pallas_call(
        _rope_table_kernel,
        out_shape=(shp, shp, shp),
        grid=(m // tm,),
        in_specs=[pl.BlockSpec((tm, 1), lambda i: (i, 0)), pl.BlockSpec((1, LANES), lambda i: (0, 0))],
        out_specs=(out, out, out),
        compiler_params=_params("parallel"),
        name="rope_tables",
    )(pos, inv)


def _rotate(x, same, lo, hi):
    half = ROT_DIM // 2
    x = x.astype(F32)
    return x * same + pltpu.roll(x, half, axis=1) * lo + pltpu.roll(x, LANES - half, axis=1) * hi


def _attn_kernel(lam_ref, q_ref, k_ref, v_ref, same_ref, lo_ref, hi_ref, nw_ref, o_ref,
                 krot, vext, s_buf, m_s, acc_s, *, lam_init, tq, strip, heads):
    qi = pl.program_id(2)
    hw = 2 * A_DH
    lv = lam_ref[...]
    lam = (jnp.exp(jnp.sum(lv[0:1] * lv[1:2], -1, keepdims=True))
           - jnp.exp(jnp.sum(lv[2:3] * lv[3:4], -1, keepdims=True)) + lam_init)
    head_lanes = [slice(g * hw, (g + 1) * hw) for g in range(heads)]

    @pl.when(qi == 0)
    def _():
        for g, hl in enumerate(head_lanes):
            krot[:, hl] = _rotate(k_ref[:, hl], same_ref[...], lo_ref[...], hi_ref[...]).astype(BF16)
            vext[g, :, :hw] = v_ref[:, hl]
            vext[g, :, hw:] = jnp.ones((vext.shape[1], hw), BF16)

    q0 = pl.multiple_of(qi * tq, tq)
    tabs = [t[pl.ds(q0, tq), :] for t in (same_ref, lo_ref, hi_ref)]
    lane = lax.broadcasted_iota(jnp.int32, (tq, hw), 1)
    qq = []
    for hl in head_lanes:
        q = (_rotate(q_ref[:, hl], *tabs) * (A_DH ** -0.5 * LOG2_E)).astype(BF16)
        zero = jnp.zeros_like(q)
        qq.append(jnp.concatenate([jnp.where(lane < A_DH, q, zero), jnp.where(lane >= A_DH, q, zero)],
                                  axis=0))
    m_s[...] = jnp.full(m_s.shape, NEG_BIG, F32)
    acc_s[...] = jnp.zeros_like(acc_s)

    strips = [slice(r, r + strip) for r in range(0, 2 * tq, strip)]

    def scores(g, j, rows):
        r0 = pl.multiple_of(j * tq, tq)
        return lax.dot_general(qq[g][rows], krot[pl.ds(r0, tq), head_lanes[g]], NT_DIMS,
                               preferred_element_type=F32)

    def block(j, diagonal):
        r0 = pl.multiple_of(j * tq, tq)
        for rows in strips:
            for g in range(heads):
                if diagonal:
                    nk = (rows.start & (tq - 1)) + strip
                    s = s_buf[g, rows, :nk]
                    row = (lax.broadcasted_iota(jnp.int32, s.shape, 0) + rows.start) & (tq - 1)
                    col = lax.broadcasted_iota(jnp.int32, s.shape, 1)
                    s = jnp.where(col <= row, s, NEG_BIG)
                else:
                    nk = tq
                    s = s_buf[g, rows, :]
                    s_buf[g, rows, :] = scores(g, j + 1, rows)
                m_prev = m_s[g, rows, :]
                m_new = jnp.maximum(m_prev, jnp.max(s, -1, keepdims=True))
                p = jnp.exp2(s - jnp.concatenate([m_new] * (nk // LANES), axis=1))
                a = jnp.exp2(m_prev - m_new)
                acc_s[g, rows, :] = (jnp.concatenate([a, a], axis=1) * acc_s[g, rows, :]
                                     + jnp.dot(p.astype(BF16), vext[g, pl.ds(r0, nk), :],
                                               preferred_element_type=F32))
                m_s[g, rows, :] = m_new

    def off_diagonal(j, carry):
        block(j, False)
        return carry

    for rows in strips:
        for g in range(heads):
            s_buf[g, rows, :] = scores(g, 0, rows)
    lax.fori_loop(0, qi, off_diagonal, 0)
    block(qi, True)

    for g, hl in enumerate(head_lanes):
        acc = acc_s[g]
        o = acc[:, :hw] / acc[:, hw:]
        o = o[:tq] - lam * o[tq:]
        o_ref[:, hl] = (_rms_norm(o, nw_ref[:, hl]) * (1.0 - lam_init)).astype(o_ref.dtype)


def _attention(proj, tables, lam_vecs, norm_w, lam_init, batch, seq, tq=512, strip=128, heads=2):
    nq = seq // tq
    hw = 2 * A_DH
    gw = heads * hw
    kv = lambda off: pl.BlockSpec((seq, gw), lambda b, h, i: (b, off // gw + h))
    tab = pl.BlockSpec((seq, LANES), lambda b, h, i: (b, 0))
    return pl.pallas_call(
        functools.partial(_attn_kernel, lam_init=lam_init, tq=tq, strip=min(strip, 2 * tq), heads=heads),
        out_shape=jax.ShapeDtypeStruct((batch * seq, A_W), BF16),
        grid=(batch, A_HEADS // heads, nq),
        in_specs=[pl.BlockSpec((4, A_DH), lambda b, h, i: (0, 0)),
                  pl.BlockSpec((tq, gw), lambda b, h, i: (b * nq + i, OFF_AQ // gw + h)),
                  kv(OFF_AK), kv(OFF_AV), tab, tab, tab,
                  pl.BlockSpec((1, gw), lambda b, h, i: (0, h))],
        out_specs=pl.BlockSpec((tq, gw), lambda b, h, i: (b * nq + i, h)),
        scratch_shapes=[pltpu.VMEM((seq, gw), BF16), pltpu.VMEM((heads, seq, 2 * hw), BF16),
                        pltpu.VMEM((heads, 2 * tq, tq), F32), pltpu.VMEM((heads, 2 * tq, LANES), F32),
                        pltpu.VMEM((heads, 2 * tq, 2 * hw), F32)],
        compiler_params=_params("parallel", "parallel", "arbitrary"),
        name="diff_attn",
    )(lam_vecs, proj, proj, proj, *tables, norm_w.reshape(1, -1))


def _hgrn_kernel(lbl_ref, q_ref, f_ref, v_ref, og_ref, nw_ref, out_ref, qs, ks, bs, st_s, *,
                 layer, heads):
    blk = pl.program_id(2)
    tb = q_ref.shape[0]
    nchunk = tb // CHUNK
    mid = CHUNK // 2 - 1
    head_lanes = [slice(g * G_DK, (g + 1) * G_DK) for g in range(heads)]

    @pl.when(blk == 0)
    def _():
        st_s[...] = jnp.zeros_like(st_s)

    logits = lbl_ref[...]
    e = jnp.exp(logits - jnp.max(logits, 0, keepdims=True))
    p = e / jnp.sum(e, 0, keepdims=True)
    lb = jnp.sum(p[0:layer + 1], 0, keepdims=True) - p[0:1]

    f_g = lb + (1.0 - lb) * _sigmoid(f_ref[...].astype(F32))
    bs[...] = _chunk_cumsum(jnp.log(jnp.maximum(f_g, F_FLOOR)), CHUNK)
    ks[...] = 1.0 - f_g
    qv = q_ref[...].astype(F32)
    qs[...] = qv * _gate_sigmoid(qv)

    def finish(r0, hl, o):
        gg = og_ref[pl.ds(r0, CHUNK), hl].astype(F32)
        hn = _rms_norm(o, nw_ref[:, hl]) * (gg * _gate_sigmoid(gg))
        out_ref[pl.ds(r0, CHUNK), hl] = hn.astype(out_ref.dtype)

    def state_terms(bc, qc, kc, vb, st):
        b_last = bc[CHUNK - 1:CHUNK]
        o_inter = lax.dot_general((qc * jnp.exp(bc)).astype(BF16), st.astype(BF16), NT_DIMS,
                                  preferred_element_type=F32)
        k_dec = (kc * jnp.exp(b_last - bc)).astype(BF16)
        st_new = st * jnp.exp(b_last) + lax.dot_general(vb, k_dec, TN_DIMS, preferred_element_type=F32)
        return o_inter, st_new

    spread = jnp.zeros((1, bs.shape[1]), F32)
    for c in range(nchunk):
        b_mid = bs[c * CHUNK + mid:c * CHUNK + mid + 1, :]
        spread = jnp.maximum(spread, jnp.maximum(bs[c * CHUNK:c * CHUNK + 1, :] - b_mid,
                                                 b_mid - bs[(c + 1) * CHUNK - 1:(c + 1) * CHUNK, :]))
    narrow = jnp.max(spread) <= FACTOR_SPREAD

    @pl.when(narrow)
    def _():
        ri = lax.broadcasted_iota(jnp.int32, (CHUNK, CHUNK), 0)
        ci = lax.broadcasted_iota(jnp.int32, (CHUNK, CHUNK), 1)
        causal = ri >= ci
        st = [st_s[g] for g in range(heads)]
        for c in range(nchunk):
            r0 = c * CHUNK
            for g, hl in enumerate(head_lanes):
                bc, qc, kc = bs[r0:r0 + CHUNK, hl], qs[r0:r0 + CHUNK, hl], ks[r0:r0 + CHUNK, hl]
                vb = v_ref[r0:r0 + CHUNK, hl]
                b_mid = bc[mid:mid + 1]
                q_sc = (qc * jnp.exp(bc - b_mid)).astype(BF16)
                k_sc = (kc * jnp.exp(b_mid - bc)).astype(BF16)
                a = lax.dot_general(q_sc, k_sc, NT_DIMS, preferred_element_type=F32)
                a = jnp.where(causal, a, 0.0).astype(BF16)
                o_inter, st[g] = state_terms(bc, qc, kc, vb, st[g])
                finish(r0, hl, o_inter + jnp.dot(a, vb, preferred_element_type=F32))
        for g in range(heads):
            st_s[g] = st[g]

    @pl.when(jnp.logical_not(narrow))
    def _():
        row_c = lax.broadcasted_iota(jnp.int32, (CHUNK, G_DK), 0)
        row_s = lax.broadcasted_iota(jnp.int32, (SUB, G_DK), 0)

        def chunk_step(c, carry):
            r0 = pl.multiple_of(c * CHUNK, CHUNK)
            for g, hl in enumerate(head_lanes):
                bc = bs[pl.ds(r0, CHUNK), hl]
                qc = qs[pl.ds(r0, CHUNK), hl]
                kc = ks[pl.ds(r0, CHUNK), hl]
                vb = v_ref[pl.ds(r0, CHUNK), hl]
                vc = vb.astype(F32)
                o_inter, st_new = state_terms(bc, qc, kc, vb, st_s[g])
                outs = []
                for i in range(CHUNK // SUB):
                    r = slice(i * SUB, (i + 1) * SUB)
                    b_i, q_i, k_i, v_i = bc[r], qc[r], kc[r], vc[r]
                    o_i = o_inter[r]
                    if i > 0:
                        c_i = b_i[0:1]
                        q_sc = (q_i * jnp.exp(b_i - c_i)).astype(BF16)
                        k_sc = jnp.where(row_c < i * SUB, kc * jnp.exp(jnp.minimum(c_i - bc, 0.0)), 0.0)
                        a_off = lax.dot_general(q_sc, k_sc.astype(BF16), NT_DIMS,
                                                preferred_element_type=F32)
                        o_i = o_i + jnp.dot(a_off.astype(BF16), vb, preferred_element_type=F32)
                    for s in range(SUB):
                        dec = jnp.exp(jnp.minimum(b_i - b_i[s:s + 1], 0.0))
                        a_s = jnp.sum(jnp.where(row_s >= s, q_i * (k_i[s:s + 1] * dec), 0.0), -1,
                                      keepdims=True)
                        o_i = o_i + a_s * v_i[s:s + 1]
                    outs.append(o_i)
                st_s[g] = st_new
                finish(r0, hl, jnp.concatenate(outs, axis=0))
            return carry

        lax.fori_loop(0, nchunk, chunk_step, 0)


def _hgrn(proj, lb_logits, norm_w, layer, batch, seq, tb=1024, heads=2):
    nblk = seq // tb
    w = heads * G_DK
    col = lambda off: (lambda b, h, s: (b * nblk + s, off // w + h))
    vec = pl.BlockSpec((1, w), lambda b, h, s: (0, h))
    return pl.pallas_call(
        functools.partial(_hgrn_kernel, layer=layer, heads=heads),
        out_shape=jax.ShapeDtypeStruct((batch * seq, G_WV), BF16),
        grid=(batch, G_HEADS // heads, nblk),
        in_specs=[pl.BlockSpec((DEPTH, w), lambda b, h, s: (0, h)),
                  pl.BlockSpec((tb, w), col(OFF_GQ)),
                  pl.BlockSpec((tb, w), col(OFF_GF)),
                  pl.BlockSpec((tb, w), col(OFF_GI)),
                  pl.BlockSpec((tb, w), col(OFF_GG)),
                  vec],
        out_specs=pl.BlockSpec((tb, w), lambda b, h, s: (b * nblk + s, h)),
        scratch_shapes=[pltpu.VMEM((tb, w), F32), pltpu.VMEM((tb, w), F32), pltpu.VMEM((tb, w), F32),
                        pltpu.VMEM((heads, G_DK, G_DK), F32)],
        compiler_params=_params("parallel", "parallel", "arbitrary"),
        name="hgrn2",
    )(lb_logits, proj, proj, proj, proj, norm_w.reshape(1, -1))


def _mix_kernel(hm_ref, ha_ref, hg_ref, g0_ref, g1_ref, g2_ref, pm_ref, pa_ref, pg_ref, wo_ref,
                h_ref, lw_ref, lb_ref, o_ref, obf_ref):
    def branch(x_ref, p_ref, g_ref):
        return (_gate_sigmoid(g_ref[...].astype(F32))
                * jnp.dot(x_ref[...], p_ref[...], preferred_element_type=F32))

    y = branch(hm_ref, pm_ref, g0_ref) + branch(ha_ref, pa_ref, g1_ref) + branch(hg_ref, pg_ref, g2_ref)
    mix = jnp.dot(y.astype(BF16), wo_ref[...], preferred_element_type=F32)
    h = _layer_norm(ALPHA * h_ref[...] + mix, lw_ref[...], lb_ref[...])
    o_ref[...] = h
    obf_ref[...] = h.astype(BF16)


def _mix_ln(hm, ha, hg, gate_pre, p_m, p_a, p_g, w_out, layer, h, ln_w, ln_b, tm=256):
    m, d = h.shape
    row = pl.BlockSpec((tm, d), lambda i: (i, 0))
    vec = pl.BlockSpec((1, d), lambda i: (0, 0))
    act = lambda w: pl.BlockSpec((tm, w), lambda i: (i, 0))
    gate = lambda br: pl.BlockSpec((tm, d), lambda i: (i, br))
    wgt = lambda w: pl.BlockSpec((None,) + w.shape[1:], lambda i: (layer, 0, 0),
                                 pipeline_mode=pl.Buffered(1))
    return pl.pallas_call(
        _mix_kernel,
        out_shape=(jax.ShapeDtypeStruct((m, d), F32), jax.ShapeDtypeStruct((m, d), BF16)),
        grid=(m // tm,),
        in_specs=[act(M_W), act(A_W), act(G_WV), gate(0), gate(1), gate(2),
                  wgt(p_m), wgt(p_a), wgt(p_g), wgt(w_out), row, vec, vec],
        out_specs=(row, row),
        compiler_params=_params("parallel"),
        name="mix_ln",
    )(hm, ha, hg, gate_pre, gate_pre, gate_pre, p_m, p_a, p_g, w_out, h,
      ln_w.reshape(1, d), ln_b.reshape(1, d))


def _mlp_kernel(hbf_ref, wu_ref, wd_ref, h_ref, lw_ref, lb_ref, o_ref, obf_ref):
    j = pl.program_id(1)

    @pl.when(j == 0)
    def _():
        o_ref[...] = jnp.zeros_like(o_ref)

    up = jnp.maximum(jnp.dot(hbf_ref[...], wu_ref[...], preferred_element_type=F32), 0.0)
    o_ref[...] += jnp.dot((up * up).astype(BF16), wd_ref[...], preferred_element_type=F32)

    @pl.when(j == pl.num_programs(1) - 1)
    def _():
        h = _layer_norm(ALPHA * h_ref[...] + o_ref[...], lw_ref[...], lb_ref[...])
        o_ref[...] = h
        obf_ref[...] = h.astype(BF16)


def _mlp_ln(hbf, w_up, w_down, layer, h, ln_w, ln_b, tm=512, tf=1024):
    m, d = h.shape
    f = w_up.shape[2]
    row = pl.BlockSpec((tm, d), lambda i, j: (i, 0))
    vec = pl.BlockSpec((1, d), lambda i, j: (0, 0))
    return pl.pallas_call(
        _mlp_kernel,
        out_shape=(jax.ShapeDtypeStruct((m, d), F32), jax.ShapeDtypeStruct((m, d), BF16)),
        grid=(m // tm, f // tf),
        in_specs=[row, pl.BlockSpec((None, d, tf), lambda i, j: (layer, 0, j)),
                  pl.BlockSpec((None, tf, d), lambda i, j: (layer, j, 0)), row, vec, vec],
        out_specs=(row, row),
        compiler_params=_params("parallel", "arbitrary"),
        name="mlp_ln",
    )(hbf, w_up, w_down, h, ln_w.reshape(1, d), ln_b.reshape(1, d))


def kernel(x, positions, ln0_w, ln0_b, w_in, m_conv_w, m_conv_b, m_gate_b, m_norm_w, a_lambda, a_norm_w,
           g_lb_logits, g_norm_w, p_m, p_a, p_g, w_out, ln1_w, ln1_b, w_up, w_down, ln2_w, ln2_b):
    batch, seq, d = x.shape
    m = batch * seq
    pos = positions.reshape(m, 1)
    inv = jnp.power(jnp.float32(ROPE_THETA), -jnp.arange(0, ROT_DIM, 2, dtype=F32) / ROT_DIM)
    inv = jnp.tile(inv, LANES // inv.shape[0]).reshape(1, LANES)

    w_t = jnp.swapaxes(w_in, 1, 2)
    row_r = PROJ_M_W + 2 * M_HEADS
    p_m, p_a, p_g, w_out, w_up, w_down = (t.astype(BF16) for t in (p_m, p_a, p_g, w_out, w_up, w_down))
    gate_b = jnp.pad(m_gate_b, ((0, 0), (0, LANES - 2 * M_HEADS)))

    tables = _rope_tables(pos, inv)
    h, hbf = _ln(x.reshape(m, d), ln0_w, ln0_b)
    for l in range(DEPTH):
        proj_m = _in_proj(hbf, w_t, l, 0, PROJ_M_W, BF16, 2048, 512, "in_proj_m")
        proj_r = _in_proj(hbf, w_t, l, row_r, OFF_GATE, BF16, 2048, 512, "in_proj_r")
        proj_g = _in_proj(hbf, w_t, l, row_r + OFF_GATE, N_BRANCH * D_MODEL, BF16, 2048, 512, "in_proj_g")
        gates = _in_proj(hbf, w_t, l, PROJ_M_W, 2 * M_HEADS, F32, 2048, LANES, "gate_proj")

        hm = _mlstm(proj_m, gates, gate_b[l:l + 1], m_conv_w[l], m_conv_b[l], m_norm_w[l], batch, seq)
        lam_init = 0.8 - 0.6 * math.exp(-0.3 * l)
        ha = _attention(proj_r, tables, a_lambda[l], a_norm_w[l], lam_init, batch, seq)
        hg = _hgrn(proj_r, g_lb_logits, g_norm_w[l], l, batch, seq)

        h, hbf = _mix_ln(hm, ha, hg, proj_g, p_m, p_a, p_g, w_out, l, h, ln1_w[l], ln1_b[l])
        h, hbf = _mlp_ln(hbf, w_up, w_down, l, h, ln2_w[l], ln2_b[l])
    return h.reshape(batch, seq, d)
```

```python
import functools
import math

import jax
import jax.numpy as jnp
from jax import lax
from jax.experimental import pallas as pl
from jax.experimental.pallas import tpu as pltpu

F32 = jnp.float32
BF16 = jnp.bfloat16

D_MODEL = 2048
DEPTH = 2
M_HEADS = 4
M_DH = 256
M_W = M_HEADS * M_DH
CONV_K = 4
A_HEADS = 8
A_DH = 64
A_W = A_HEADS * 2 * A_DH
ROT_DIM = A_DH // 4
ROPE_THETA = 500000.0
G_HEADS = 8
G_DK = 128
G_WK = G_HEADS * G_DK
G_WV = G_HEADS * G_DK
CHUNK = 64
M_CHUNK = 256
SUB = 16
FACTOR_SPREAD = 60.0
N_BRANCH = 3
D_FF = 4 * D_MODEL
ALPHA = (2 * DEPTH) ** 0.25
LN_EPS = 1e-5
NORM_EPS = 1e-6
NEG_BIG = -1e30
LOG2_E = math.log2(math.e)
F_FLOOR = 1e-30

OFF_MQ, OFF_MK, OFF_MV, OFF_MO = 0, M_W, 2 * M_W, 3 * M_W
PROJ_M_W = 4 * M_W
OFF_AQ = 0
OFF_AK = OFF_AQ + A_W
OFF_AV = OFF_AK + A_W
OFF_GQ = OFF_AV + A_W
OFF_GF = OFF_GQ + G_WK
OFF_GI = OFF_GF + G_WK
OFF_GG = OFF_GI + G_WV
OFF_GATE = OFF_GG + G_WV
PROJ_R_W = OFF_GATE + N_BRANCH * D_MODEL
LANES = 128
VMEM_LIMIT = 56 * 1024 * 1024

NT_DIMS = (((1,), (1,)), ((), ()))
TN_DIMS = (((0,), (0,)), ((), ()))


def _params(*sem):
    return pltpu.CompilerParams(dimension_semantics=sem, vmem_limit_bytes=VMEM_LIMIT)


def _sigmoid(x):
    return 1.0 / (1.0 + jnp.exp(-x))


def _gate_sigmoid(x):
    return 0.5 * jnp.tanh(0.5 * x) + 0.5


def _layer_norm(x, w, b):
    mu = jnp.mean(x, -1, keepdims=True)
    xc = x - mu
    var = jnp.mean(xc * xc, -1, keepdims=True)
    return xc * lax.rsqrt(var + LN_EPS) * w + b


def _rms_norm(x, w):
    return x * lax.rsqrt(jnp.mean(x * x, -1, keepdims=True) + NORM_EPS) * w


def _chunk_cumsum(x, chunk):
    row = lax.broadcasted_iota(jnp.int32, x.shape, 0) & (chunk - 1)
    sh = 1
    while sh < chunk:
        x = x + jnp.where(row >= sh, pltpu.roll(x, sh, axis=0), 0.0)
        sh *= 2
    return x


def _cast_riders(weights, layer, steps, step_index):
    ins = [pl.BlockSpec((None, w.shape[1] // steps, w.shape[2]), lambda *g: (layer, step_index(*g), 0))
           for w in weights]
    outs = [pl.BlockSpec((w.shape[1] // steps, w.shape[2]), lambda *g: (step_index(*g), 0)) for w in weights]
    shapes = [jax.ShapeDtypeStruct(w.shape[1:], BF16) for w in weights]
    return ins, outs, shapes


def _cast_slabs(srcs, dsts):
    for src, dst in zip(srcs, dsts):
        dst[...] = src[...].astype(dst.dtype)


def _ln_kernel(x_ref, w_ref, b_ref, o_ref, obf_ref):
    y = _layer_norm(x_ref[...], w_ref[...], b_ref[...])
    o_ref[...] = y
    obf_ref[...] = y.astype(BF16)


def _ln(x, w, b, tm=256):
    m, d = x.shape
    return pl.pallas_call(
        _ln_kernel,
        out_shape=(jax.ShapeDtypeStruct((m, d), F32), jax.ShapeDtypeStruct((m, d), BF16)),
        grid=(m // tm,),
        in_specs=[pl.BlockSpec((tm, d), lambda i: (i, 0)),
                  pl.BlockSpec((1, d), lambda i: (0, 0)),
                  pl.BlockSpec((1, d), lambda i: (0, 0))],
        out_specs=(pl.BlockSpec((tm, d), lambda i: (i, 0)),
                   pl.BlockSpec((tm, d), lambda i: (i, 0))),
        compiler_params=_params("parallel"),
        name="ln0",
    )(x, w.reshape(1, d), b.reshape(1, d))


def _proj_kernel(a_ref, w_ref, *rest, shift, pad):
    o_ref = rest[-1]
    w = w_ref[...]
    if shift:
        w = jnp.concatenate([w[shift:], rest[0][...]], axis=0)
    if pad:
        w = jnp.concatenate([w, jnp.zeros((pad, w.shape[1]), w.dtype)], axis=0)
    o_ref[...] = lax.dot_general(a_ref[...], w.astype(BF16), NT_DIMS,
                                 preferred_element_type=F32).astype(o_ref.dtype)


def _in_proj(a, w_t, layer, row0, n, out_dtype, tm, tn, name):
    m, k = a.shape
    shift = row0 % tn
    base = row0 - shift
    rows = min(tn, n)
    pad = tn - rows
    in_specs = [pl.BlockSpec((tm, k), lambda i, j: (i, 0)),
                pl.BlockSpec((None, rows, k), lambda i, j: (layer, base // rows + j, 0))]
    if shift:
        in_specs.append(pl.BlockSpec((None, shift, k),
                                     lambda i, j: (layer, (base + (j + 1) * tn) // shift, 0)))
    return pl.pallas_call(
        functools.partial(_proj_kernel, shift=shift, pad=pad),
        out_shape=jax.ShapeDtypeStruct((m, max(n, tn)), out_dtype),
        grid=(m // tm, max(n // tn, 1)),
        in_specs=in_specs,
        out_specs=pl.BlockSpec((tm, tn), lambda i, j: (i, j)),
        compiler_params=_params("parallel", "arbitrary"),
        name=name,
    )(*([a] + [w_t] * (len(in_specs) - 1)))


def _mlstm_kernel(q_ref, k_ref, v_ref, og_ref, g_ref, gb_ref, cw_ref, cb_ref, nw_ref, *rest, n_cast):
    out_ref = rest[n_cast]
    qpad, kpad, c_s, n_s, m_s = rest[2 * n_cast + 1:]
    _cast_slabs(rest[:n_cast], rest[n_cast + 1:2 * n_cast + 1])
    blk = pl.program_id(1)
    tb = q_ref.shape[0]

    @pl.when(blk == 0)
    def _():
        qpad[0:8, :] = jnp.zeros((8, M_W), F32)
        kpad[0:8, :] = jnp.zeros((8, M_W), F32)
        c_s[...] = jnp.zeros_like(c_s)
        n_s[...] = jnp.zeros_like(n_s)
        m_s[...] = jnp.zeros_like(m_s)

    @pl.when(blk > 0)
    def _():
        qpad[0:8, :] = qpad[tb:tb + 8, :]
        kpad[0:8, :] = kpad[tb:tb + 8, :]

    qpad[8:8 + tb, :] = q_ref[...].astype(F32)
    kpad[8:8 + tb, :] = k_ref[...].astype(F32)

    def conv_silu(pad, cols, wofs):
        wcols = slice(wofs + cols.start, wofs + cols.stop)
        y = cb_ref[:, wcols]
        for j in range(CONV_K):
            off = 8 - (CONV_K - 1) + j
            y = y + cw_ref[j:j + 1, wcols] * pad[off:off + tb, cols]
        return y * _gate_sigmoid(y)

    ri = lax.broadcasted_iota(jnp.int32, (tb, tb), 0)
    ci = lax.broadcasted_iota(jnp.int32, (tb, tb), 1)
    causal = ri >= ci

    gb = g_ref[...] + gb_ref[...]
    log_f = jnp.minimum(gb, 0.0) - jnp.log(1.0 + jnp.exp(-jnp.abs(gb)))
    bcum = _chunk_cumsum(log_f, tb)
    lane = lax.broadcasted_iota(jnp.int32, gb.shape, 1)
    pc = jnp.where(lane < M_HEADS, gb, bcum)
    pt = pc.T

    for h in range(M_HEADS):
        cols = slice(h * M_DH, (h + 1) * M_DH)
        qc = conv_silu(qpad, cols, 0)
        kc = conv_silu(kpad, cols, M_W) * (M_DH ** -0.5)
        vc = v_ref[:, cols]
        qb = qc.astype(BF16)

        i_col, b_col = pc[:, h:h + 1], pc[:, M_HEADS + h:M_HEADS + h + 1]
        i_row, b_row = pt[h:h + 1, :], pt[M_HEADS + h:M_HEADS + h + 1, :]
        m_prev = m_s[h]
        c_prev = c_s[h]
        n_prev = n_s[h]

        log_d = jnp.where(causal, b_col + (i_row - b_row), NEG_BIG)
        log_inter = b_col + m_prev
        m_t = jnp.maximum(log_inter, jnp.max(log_d, -1, keepdims=True))
        w_inter = jnp.exp(log_inter - m_t)
        s = lax.dot_general(qb, kc.astype(BF16), NT_DIMS, preferred_element_type=F32)
        s = s * jnp.exp(log_d - m_t)
        num = (jnp.dot(s.astype(BF16), vc, preferred_element_type=F32)
               + w_inter * jnp.dot(qb, c_prev.astype(BF16), preferred_element_type=F32))
        den = jnp.sum(s, -1, keepdims=True) + w_inter * jnp.sum(qc * n_prev, -1, keepdims=True)
        hh = num / jnp.maximum(jnp.abs(den), jnp.exp(-m_t))

        g = b_col[tb - 1:tb, :]
        log_w = g - b_col + i_col
        m_new = jnp.maximum(g + m_prev, jnp.max(log_w, 0, keepdims=True))
        decay = jnp.exp(g + m_prev - m_new)
        kw = kc * jnp.exp(log_w - m_new)
        c_s[h] = decay * c_prev + lax.dot_general(kw.astype(BF16), vc, TN_DIMS,
                                                  preferred_element_type=F32)
        n_s[h] = decay * n_prev + jnp.sum(kw, 0, keepdims=True)
        m_s[h] = m_new

        hn = _rms_norm(hh, nw_ref[:, cols]) * _gate_sigmoid(og_ref[:, cols].astype(F32))
        out_ref[:, cols] = hn.astype(out_ref.dtype)


def _mlstm(proj, gates, gate_b, conv_w, conv_b, norm_w, layer, cast_weights, batch, seq, tb=M_CHUNK):
    nblk = seq // tb
    col = lambda off: pl.BlockSpec((tb, M_W), lambda b, s: (b * nblk + s, off // M_W))
    full = lambda a: pl.BlockSpec(a.shape, lambda b, s: (0, 0))
    conv_b = conv_b.reshape(1, -1)
    norm_w = norm_w.reshape(1, -1)
    c_in, c_out, c_shapes = _cast_riders(cast_weights, layer, batch * nblk, lambda b, s: b * nblk + s)
    return pl.pallas_call(
        functools.partial(_mlstm_kernel, n_cast=len(cast_weights)),
        out_shape=[jax.ShapeDtypeStruct((batch * seq, M_W), BF16)] + c_shapes,
        grid=(batch, nblk),
        in_specs=[col(OFF_MQ), col(OFF_MK), col(OFF_MV), col(OFF_MO),
                  pl.BlockSpec((tb, LANES), lambda b, s: (b * nblk + s, 0)),
                  full(gate_b), full(conv_w), full(conv_b), full(norm_w)] + c_in,
        out_specs=[pl.BlockSpec((tb, M_W), lambda b, s: (b * nblk + s, 0))] + c_out,
        scratch_shapes=[pltpu.VMEM((tb + 8, M_W), F32), pltpu.VMEM((tb + 8, M_W), F32),
                        pltpu.VMEM((M_HEADS, M_DH, M_DH), F32), pltpu.VMEM((M_HEADS, 1, M_DH), F32),
                        pltpu.VMEM((M_HEADS, 1, 1), F32)],
        compiler_params=_params("parallel", "arbitrary"),
        name="mlstm",
    )(proj, proj, proj, proj, gates, gate_b, conv_w, conv_b, norm_w, *cast_weights)


def _rope_table_kernel(pos_ref, inv_ref, same_ref, lo_ref, hi_ref):
    ang = pos_ref[...].astype(F32) * inv_ref[...]
    lane = lax.broadcasted_iota(jnp.int32, ang.shape, 1) & (A_DH - 1)
    half = ROT_DIM // 2
    cos, sin = jnp.cos(ang), jnp.sin(ang)
    same_ref[...] = jnp.where(lane < ROT_DIM, cos, 1.0)
    lo_ref[...] = jnp.where((lane >= half) & (lane < ROT_DIM), sin, 0.0)
    hi_ref[...] = jnp.where(lane < half, -sin, 0.0)


def _rope_tables(pos, inv, tm=512):
    m = pos.shape[0]
    out = pl.BlockSpec((tm, LANES), lambda i: (i, 0))
    shp = jax.ShapeDtypeStruct((m, LANES), F32)
    return pl.pallas_call(
        _rope_table_kernel,
        out_shape=(shp, shp, shp),
        grid=(m // tm,),
        in_specs=[pl.BlockSpec((tm, 1), lambda i: (i, 0)), pl.BlockSpec((1, LANES), lambda i: (0, 0))],
        out_specs=(out, out, out),
        compiler_params=_params("parallel"),
        name="rope_tables",
    )(pos, inv)


def _rotate(x, same, lo, hi):
    half = ROT_DIM // 2
    x = x.astype(F32)
    return x * same + pltpu.roll(x, half, axis=1) * lo + pltpu.roll(x, LANES - half, axis=1) * hi


def _attn_kernel(lam_ref, q_ref, k_ref, v_ref, same_ref, lo_ref, hi_ref, nw_ref, o_ref,
                 krot, vext, s_buf, m_s, acc_s, *, lam_init, tq, strip, heads):
    qi = pl.program_id(2)
    hw = 2 * A_DH
    lv = lam_ref[...]
    lam = (jnp.exp(jnp.sum(lv[0:1] * lv[1:2], -1, keepdims=True))
           - jnp.exp(jnp.sum(lv[2:3] * lv[3:4], -1, keepdims=True)) + lam_init)
    head_lanes = [slice(g * hw, (g + 1) * hw) for g in range(heads)]

    @pl.when(qi == 0)
    def _():
        for g, hl in enumerate(head_lanes):
            krot[:, hl] = _rotate(k_ref[:, hl], same_ref[...], lo_ref[...], hi_ref[...]).astype(BF16)
            vext[g, :, :hw] = v_ref[:, hl]
            vext[g, :, hw:] = jnp.ones((vext.shape[1], hw), BF16)

    q0 = pl.multiple_of(qi * tq, tq)
    tabs = [t[pl.ds(q0, tq), :] for t in (same_ref, lo_ref, hi_ref)]
    lane = lax.broadcasted_iota(jnp.int32, (tq, hw), 1)
    qq = []
    for hl in head_lanes:
        q = (_rotate(q_ref[:, hl], *tabs) * (A_DH ** -0.5 * LOG2_E)).astype(BF16)
        zero = jnp.zeros_like(q)
        qq.append(jnp.concatenate([jnp.where(lane < A_DH, q, zero), jnp.where(lane >= A_DH, q, zero)],
                                  axis=0))
    m_s[...] = jnp.full(m_s.shape, NEG_BIG, F32)
    acc_s[...] = jnp.zeros_like(acc_s)

    strips = [slice(r, r + strip) for r in range(0, 2 * tq, strip)]

    def scores(g, j, rows):
        r0 = pl.multiple_of(j * tq, tq)
        return lax.dot_general(qq[g][rows], krot[pl.ds(r0, tq), head_lanes[g]], NT_DIMS,
                               preferred_element_type=F32)

    def block(j, diagonal):
        r0 = pl.multiple_of(j * tq, tq)
        for rows in strips:
            for g in range(heads):
                if diagonal:
                    nk = (rows.start & (tq - 1)) + strip
                    s = s_buf[g, rows, :nk]
                    row = (lax.broadcasted_iota(jnp.int32, s.shape, 0) + rows.start) & (tq - 1)
                    col = lax.broadcasted_iota(jnp.int32, s.shape, 1)
                    s = jnp.where(col <= row, s, NEG_BIG)
                else:
                    nk = tq
                    s = s_buf[g, rows, :]
                    s_buf[g, rows, :] = scores(g, j + 1, rows)
                m_prev = m_s[g, rows, :]
                m_new = jnp.maximum(m_prev, jnp.max(s, -1, keepdims=True))
                p = jnp.exp2(s - jnp.concatenate([m_new] * (nk // LANES), axis=1))
                a = jnp.exp2(m_prev - m_new)
                acc_s[g, rows, :] = (jnp.concatenate([a, a], axis=1) * acc_s[g, rows, :]
                                     + jnp.dot(p.astype(BF16), vext[g, pl.ds(r0, nk), :],
                                               preferred_element_type=F32))
                m_s[g, rows, :] = m_new

    def off_diagonal(j, carry):
        block(j, False)
        return carry

    for rows in strips:
        for g in range(heads):
            s_buf[g, rows, :] = scores(g, 0, rows)
    lax.fori_loop(0, qi, off_diagonal, 0)
    block(qi, True)

    for g, hl in enumerate(head_lanes):
        acc = acc_s[g]
        o = acc[:, :hw] / acc[:, hw:]
        o = o[:tq] - lam * o[tq:]
        o_ref[:, hl] = (_rms_norm(o, nw_ref[:, hl]) * (1.0 - lam_init)).astype(o_ref.dtype)


def _attention(proj, tables, lam_vecs, norm_w, lam_init, batch, seq, tq=512, strip=128, heads=2):
    nq = seq // tq
    hw = 2 * A_DH
    gw = heads * hw
    kv = lambda off: pl.BlockSpec((seq, gw), lambda b, h, i: (b, off // gw + h))
    tab = pl.BlockSpec((seq, LANES), lambda b, h, i: (b, 0))
    return pl.pallas_call(
        functools.partial(_attn_kernel, lam_init=lam_init, tq=tq, strip=min(strip, 2 * tq), heads=heads),
        out_shape=jax.ShapeDtypeStruct((batch * seq, A_W), BF16),
        grid=(batch, A_HEADS // heads, nq),
        in_specs=[pl.BlockSpec((4, A_DH), lambda b, h, i: (0, 0)),
                  pl.BlockSpec((tq, gw), lambda b, h, i: (b * nq + i, OFF_AQ // gw + h)),
                  kv(OFF_AK), kv(OFF_AV), tab, tab, tab,
                  pl.BlockSpec((1, gw), lambda b, h, i: (0, h))],
        out_specs=pl.BlockSpec((tq, gw), lambda b, h, i: (b * nq + i, h)),
        scratch_shapes=[pltpu.VMEM((seq, gw), BF16), pltpu.VMEM((heads, seq, 2 * hw), BF16),
                        pltpu.VMEM((heads, 2 * tq, tq), F32), pltpu.VMEM((heads, 2 * tq, LANES), F32),
                        pltpu.VMEM((heads, 2 * tq, 2 * hw), F32)],
        compiler_params=_params("parallel", "parallel", "arbitrary"),
        name="diff_attn",
    )(lam_vecs, proj, proj, proj, *tables, norm_w.reshape(1, -1))


def _hgrn_kernel(lbl_ref, q_ref, f_ref, v_ref, og_ref, nw_ref, *rest, layer, heads, n_cast):
    out_ref = rest[n_cast]
    qs, ks, bs, st_s = rest[2 * n_cast + 1:]
    _cast_slabs(rest[:n_cast], rest[n_cast + 1:2 * n_cast + 1])
    blk = pl.program_id(2)
    tb = q_ref.shape[0]
    nchunk = tb // CHUNK
    mid = CHUNK // 2 - 1
    head_lanes = [slice(g * G_DK, (g + 1) * G_DK) for g in range(heads)]

    @pl.when(blk == 0)
    def _():
        st_s[...] = jnp.zeros_like(st_s)

    logits = lbl_ref[...]
    e = jnp.exp(logits - jnp.max(logits, 0, keepdims=True))
    p = e / jnp.sum(e, 0, keepdims=True)
    lb = jnp.sum(p[0:layer + 1], 0, keepdims=True) - p[0:1]

    f_g = lb + (1.0 - lb) * _sigmoid(f_ref[...].astype(F32))
    bs[...] = _chunk_cumsum(jnp.log(jnp.maximum(f_g, F_FLOOR)), CHUNK)
    ks[...] = 1.0 - f_g
    qv = q_ref[...].astype(F32)
    qs[...] = qv * _gate_sigmoid(qv)

    def finish(r0, hl, o):
        gg = og_ref[pl.ds(r0, CHUNK), hl].astype(F32)
        hn = _rms_norm(o, nw_ref[:, hl]) * (gg * _gate_sigmoid(gg))
        out_ref[pl.ds(r0, CHUNK), hl] = hn.astype(out_ref.dtype)

    def state_terms(bc, qc, kc, vb, st):
        b_last = bc[CHUNK - 1:CHUNK]
        o_inter = lax.dot_general((qc * jnp.exp(bc)).astype(BF16), st.astype(BF16), NT_DIMS,
                                  preferred_element_type=F32)
        k_dec = (kc * jnp.exp(b_last - bc)).astype(BF16)
        st_new = st * jnp.exp(b_last) + lax.dot_general(vb, k_dec, TN_DIMS, preferred_element_type=F32)
        return o_inter, st_new

    spread = jnp.zeros((1, bs.shape[1]), F32)
    for c in range(nchunk):
        b_mid = bs[c * CHUNK + mid:c * CHUNK + mid + 1, :]
        spread = jnp.maximum(spread, jnp.maximum(bs[c * CHUNK:c * CHUNK + 1, :] - b_mid,
                                                 b_mid - bs[(c + 1) * CHUNK - 1:(c + 1) * CHUNK, :]))
    narrow = jnp.max(spread) <= FACTOR_SPREAD

    @pl.when(narrow)
    def _():
        ri = lax.broadcasted_iota(jnp.int32, (CHUNK, CHUNK), 0)
        ci = lax.broadcasted_iota(jnp.int32, (CHUNK, CHUNK), 1)
        causal = ri >= ci
        st = [st_s[g] for g in range(heads)]
        for c in range(nchunk):
            r0 = c * CHUNK
            for g, hl in enumerate(head_lanes):
                bc, qc, kc = bs[r0:r0 + CHUNK, hl], qs[r0:r0 + CHUNK, hl], ks[r0:r0 + CHUNK, hl]
                vb = v_ref[r0:r0 + CHUNK, hl]
                b_mid = bc[mid:mid + 1]
                q_sc = (qc * jnp.exp(bc - b_mid)).astype(BF16)
                k_sc = (kc * jnp.exp(b_mid - bc)).astype(BF16)
                a = lax.dot_general(q_sc, k_sc, NT_DIMS, preferred_element_type=F32)
                a = jnp.where(causal, a, 0.0).astype(BF16)
                o_inter, st[g] = state_terms(bc, qc, kc, vb, st[g])
                finish(r0, hl, o_inter + jnp.dot(a, vb, preferred_element_type=F32))
        for g in range(heads):
            st_s[g] = st[g]

    @pl.when(jnp.logical_not(narrow))
    def _():
        row_c = lax.broadcasted_iota(jnp.int32, (CHUNK, G_DK), 0)
        row_s = lax.broadcasted_iota(jnp.int32, (SUB, G_DK), 0)

        def chunk_step(c, carry):
            r0 = pl.multiple_of(c * CHUNK, CHUNK)
            for g, hl in enumerate(head_lanes):
                bc = bs[pl.ds(r0, CHUNK), hl]
                qc = qs[pl.ds(r0, CHUNK), hl]
                kc = ks[pl.ds(r0, CHUNK), hl]
                vb = v_ref[pl.ds(r0, CHUNK), hl]
                vc = vb.astype(F32)
                o_inter, st_new = state_terms(bc, qc, kc, vb, st_s[g])
                outs = []
                for i in range(CHUNK // SUB):
                    r = slice(i * SUB, (i + 1) * SUB)
                    b_i, q_i, k_i, v_i = bc[r], qc[r], kc[r], vc[r]
                    o_i = o_inter[r]
                    if i > 0:
                        c_i = b_i[0:1]
                        q_sc = (q_i * jnp.exp(b_i - c_i)).astype(BF16)
                        k_sc = jnp.where(row_c < i * SUB, kc * jnp.exp(jnp.minimum(c_i - bc, 0.0)), 0.0)
                        a_off = lax.dot_general(q_sc, k_sc.astype(BF16), NT_DIMS,
                                                preferred_element_type=F32)
                        o_i = o_i + jnp.dot(a_off.astype(BF16), vb, preferred_element_type=F32)
                    for s in range(SUB):
                        dec = jnp.exp(jnp.minimum(b_i - b_i[s:s + 1], 0.0))
                        a_s = jnp.sum(jnp.where(row_s >= s, q_i * (k_i[s:s + 1] * dec), 0.0), -1,
                                      keepdims=True)
                        o_i = o_i + a_s * v_i[s:s + 1]
                    outs.append(o_i)
                st_s[g] = st_new
                finish(r0, hl, jnp.concatenate(outs, axis=0))
            return carry

        lax.fori_loop(0, nchunk, chunk_step, 0)


def _hgrn(proj, lb_logits, norm_w, layer, cast_weights, batch, seq, tb=1024, heads=2):
    nblk = seq // tb
    nh = G_HEADS // heads
    w = heads * G_DK
    col = lambda off: (lambda b, h, s: (b * nblk + s, off // w + h))
    vec = pl.BlockSpec((1, w), lambda b, h, s: (0, h))
    c_in, c_out, c_shapes = _cast_riders(cast_weights, layer, batch * nh * nblk,
                                         lambda b, h, s: (b * nh + h) * nblk + s)
    return pl.pallas_call(
        functools.partial(_hgrn_kernel, layer=layer, heads=heads, n_cast=len(cast_weights)),
        out_shape=[jax.ShapeDtypeStruct((batch * seq, G_WV), BF16)] + c_shapes,
        grid=(batch, nh, nblk),
        in_specs=[pl.BlockSpec((DEPTH, w), lambda b, h, s: (0, h)),
                  pl.BlockSpec((tb, w), col(OFF_GQ)),
                  pl.BlockSpec((tb, w), col(OFF_GF)),
                  pl.BlockSpec((tb, w), col(OFF_GI)),
                  pl.BlockSpec((tb, w), col(OFF_GG)),
                  vec] + c_in,
        out_specs=[pl.BlockSpec((tb, w), lambda b, h, s: (b * nblk + s, h))] + c_out,
        scratch_shapes=[pltpu.VMEM((tb, w), F32), pltpu.VMEM((tb, w), F32), pltpu.VMEM((tb, w), F32),
                        pltpu.VMEM((heads, G_DK, G_DK), F32)],
        compiler_params=_params("parallel", "parallel", "arbitrary"),
        name="hgrn2",
    )(lb_logits, proj, proj, proj, proj, norm_w.reshape(1, -1), *cast_weights)


def _mix_kernel(hm_ref, ha_ref, hg_ref, g0_ref, g1_ref, g2_ref, pm_ref, pa_ref, pg_ref, wo_ref,
                h_ref, lw_ref, lb_ref, o_ref, obf_ref):
    def branch(x_ref, p_ref, g_ref):
        return (_gate_sigmoid(g_ref[...].astype(F32))
                * jnp.dot(x_ref[...], p_ref[...], preferred_element_type=F32))

    y = branch(hm_ref, pm_ref, g0_ref) + branch(ha_ref, pa_ref, g1_ref) + branch(hg_ref, pg_ref, g2_ref)
    mix = jnp.dot(y.astype(BF16), wo_ref[...], preferred_element_type=F32)
    h = _layer_norm(ALPHA * h_ref[...] + mix, lw_ref[...], lb_ref[...])
    o_ref[...] = h
    obf_ref[...] = h.astype(BF16)


def _mix_ln(hm, ha, hg, gate_pre, p_m, p_a, p_g, w_out, h, ln_w, ln_b, tm=256):
    m, d = h.shape
    row = pl.BlockSpec((tm, d), lambda i: (i, 0))
    vec = pl.BlockSpec((1, d), lambda i: (0, 0))
    act = lambda w: pl.BlockSpec((tm, w), lambda i: (i, 0))
    gate = lambda br: pl.BlockSpec((tm, d), lambda i: (i, br))
    wgt = lambda w: pl.BlockSpec(w.shape, lambda i: (0, 0), pipeline_mode=pl.Buffered(1))
    return pl.pallas_call(
        _mix_kernel,
        out_shape=(jax.ShapeDtypeStruct((m, d), F32), jax.ShapeDtypeStruct((m, d), BF16)),
        grid=(m // tm,),
        in_specs=[act(M_W), act(A_W), act(G_WV), gate(0), gate(1), gate(2),
                  wgt(p_m), wgt(p_a), wgt(p_g), wgt(w_out), row, vec, vec],
        out_specs=(row, row),
        compiler_params=_params("parallel"),
        name="mix_ln",
    )(hm, ha, hg, gate_pre, gate_pre, gate_pre, p_m, p_a, p_g, w_out, h,
      ln_w.reshape(1, d), ln_b.reshape(1, d))


def _mlp_kernel(hbf_ref, wu_ref, wd_ref, h_ref, lw_ref, lb_ref, o_ref, obf_ref):
    j = pl.program_id(1)

    @pl.when(j == 0)
    def _():
        o_ref[...] = jnp.zeros_like(o_ref)

    up = jnp.maximum(jnp.dot(hbf_ref[...], wu_ref[...], preferred_element_type=F32), 0.0)
    o_ref[...] += jnp.dot((up * up).astype(BF16), wd_ref[...], preferred_element_type=F32)

    @pl.when(j == pl.num_programs(1) - 1)
    def _():
        h = _layer_norm(ALPHA * h_ref[...] + o_ref[...], lw_ref[...], lb_ref[...])
        o_ref[...] = h
        obf_ref[...] = h.astype(BF16)


def _mlp_ln(hbf, w_up, w_down, h, ln_w, ln_b, tm=512, tf=1024):
    m, d = h.shape
    f = w_up.shape[1]
    row = pl.BlockSpec((tm, d), lambda i, j: (i, 0))
    vec = pl.BlockSpec((1, d), lambda i, j: (0, 0))
    return pl.pallas_call(
        _mlp_kernel,
        out_shape=(jax.ShapeDtypeStruct((m, d), F32), jax.ShapeDtypeStruct((m, d), BF16)),
        grid=(m // tm, f // tf),
        in_specs=[row, pl.BlockSpec((d, tf), lambda i, j: (0, j)),
                  pl.BlockSpec((tf, d), lambda i, j: (j, 0)), row, vec, vec],
        out_specs=(row, row),
        compiler_params=_params("parallel", "arbitrary"),
        name="mlp_ln",
    )(hbf, w_up, w_down, h, ln_w.reshape(1, d), ln_b.reshape(1, d))


def kernel(x, positions, ln0_w, ln0_b, w_in, m_conv_w, m_conv_b, m_gate_b, m_norm_w, a_lambda, a_norm_w,
           g_lb_logits, g_norm_w, p_m, p_a, p_g, w_out, ln1_w, ln1_b, w_up, w_down, ln2_w, ln2_b):
    batch, seq, d = x.shape
    m = batch * seq
    pos = positions.reshape(m, 1)
    inv = jnp.power(jnp.float32(ROPE_THETA), -jnp.arange(0, ROT_DIM, 2, dtype=F32) / ROT_DIM)
    inv = jnp.tile(inv, LANES // inv.shape[0]).reshape(1, LANES)

    w_t = jnp.swapaxes(w_in, 1, 2)
    row_r = PROJ_M_W + 2 * M_HEADS
    gate_b = jnp.pad(m_gate_b, ((0, 0), (0, LANES - 2 * M_HEADS)))

    tables = _rope_tables(pos, inv)
    h, hbf = _ln(x.reshape(m, d), ln0_w, ln0_b)
    for l in range(DEPTH):
        proj_m = _in_proj(hbf, w_t, l, 0, PROJ_M_W, BF16, 2048, 1024, "in_proj_m")
        proj_r = _in_proj(hbf, w_t, l, row_r, OFF_GATE, BF16, 2048, 1024, "in_proj_r")
        proj_g = _in_proj(hbf, w_t, l, row_r + OFF_GATE, N_BRANCH * D_MODEL, BF16, 2048, 1024, "in_proj_g")
        gates = _in_proj(hbf, w_t, l, PROJ_M_W, 2 * M_HEADS, F32, 2048, LANES, "gate_proj")

        hm, p_m_l, p_a_l, p_g_l, w_out_l = _mlstm(proj_m, gates, gate_b[l:l + 1], m_conv_w[l], m_conv_b[l],
                                                  m_norm_w[l], l, (p_m, p_a, p_g, w_out), batch, seq)
        lam_init = 0.8 - 0.6 * math.exp(-0.3 * l)
        ha = _attention(proj_r, tables, a_lambda[l], a_norm_w[l], lam_init, batch, seq)
        hg, w_up_l, w_down_l = _hgrn(proj_r, g_lb_logits, g_norm_w[l], l, (w_up, w_down), batch, seq)

        h, hbf = _mix_ln(hm, ha, hg, proj_g, p_m_l, p_a_l, p_g_l, w_out_l, h, ln1_w[l], ln1_b[l])
        h, hbf = _mlp_ln(hbf, w_up_l, w_down_l, h, ln2_w[l], ln2_b[l])
    return h.reshape(batch, seq, d)
```

```python
import functools
import math

import jax
import jax.numpy as jnp
from jax import lax
from jax.experimental import pallas as pl
from jax.experimental.pallas import tpu as pltpu

F32 = jnp.float32
BF16 = jnp.bfloat16

D_MODEL = 2048
DEPTH = 2
M_HEADS = 4
M_DH = 256
M_W = M_HEADS * M_DH
CONV_K = 4
A_HEADS = 8
A_DH = 64
A_W = A_HEADS * 2 * A_DH
ROT_DIM = A_DH // 4
ROPE_THETA = 500000.0
G_HEADS = 8
G_DK = 128
G_WK = G_HEADS * G_DK
G_WV = G_HEADS * G_DK
CHUNK = 64
M_CHUNK = 256
SUB = 16
FACTOR_SPREAD = 60.0
N_BRANCH = 3
D_FF = 4 * D_MODEL
ALPHA = (2 * DEPTH) ** 0.25
LN_EPS = 1e-5
NORM_EPS = 1e-6
NEG_BIG = -1e30
LOG2_E = math.log2(math.e)
F_FLOOR = 1e-30

OFF_MQ, OFF_MK, OFF_MV, OFF_MO = 0, M_W, 2 * M_W, 3 * M_W
PROJ_M_W = 4 * M_W
OFF_AQ = 0
OFF_AK = OFF_AQ + A_W
OFF_AV = OFF_AK + A_W
OFF_GQ = OFF_AV + A_W
OFF_GF = OFF_GQ + G_WK
OFF_GI = OFF_GF + G_WK
OFF_GG = OFF_GI + G_WV
OFF_GATE = OFF_GG + G_WV
PROJ_R_W = OFF_GATE + N_BRANCH * D_MODEL
LANES = 128
VMEM_LIMIT = 56 * 1024 * 1024

NT_DIMS = (((1,), (1,)), ((), ()))
TN_DIMS = (((0,), (0,)), ((), ()))


def _params(*sem):
    return pltpu.CompilerParams(dimension_semantics=sem, vmem_limit_bytes=VMEM_LIMIT)


def _sigmoid(x):
    return 1.0 / (1.0 + jnp.exp(-x))


def _gate_sigmoid(x):
    return 0.5 * jnp.tanh(0.5 * x) + 0.5


def _layer_norm(x, w, b):
    mu = jnp.mean(x, -1, keepdims=True)
    xc = x - mu
    var = jnp.mean(xc * xc, -1, keepdims=True)
    return xc * lax.rsqrt(var + LN_EPS) * w + b


def _rms_norm(x, w):
    return x * lax.rsqrt(jnp.mean(x * x, -1, keepdims=True) + NORM_EPS) * w


def _chunk_cumsum(x, chunk):
    row = lax.broadcasted_iota(jnp.int32, x.shape, 0) & (chunk - 1)
    sh = 1
    while sh < chunk:
        x = x + jnp.where(row >= sh, pltpu.roll(x, sh, axis=0), 0.0)
        sh *= 2
    return x


def _cast_riders(weights, layer, steps, step_index):
    ins = [pl.BlockSpec((None, w.shape[1] // steps, w.shape[2]), lambda *g: (layer, step_index(*g), 0))
           for w in weights]
    outs = [pl.BlockSpec((w.shape[1] // steps, w.shape[2]), lambda *g: (step_index(*g), 0)) for w in weights]
    shapes = [jax.ShapeDtypeStruct(w.shape[1:], BF16) for w in weights]
    return ins, outs, shapes


def _cast_slabs(srcs, dsts):
    for src, dst in zip(srcs, dsts):
        dst[...] = src[...].astype(dst.dtype)


def _ln_kernel(x_ref, w_ref, b_ref, o_ref, obf_ref):
    y = _layer_norm(x_ref[...], w_ref[...], b_ref[...])
    o_ref[...] = y
    obf_ref[...] = y.astype(BF16)


def _ln(x, w, b, tm=256):
    m, d = x.shape
    return pl.pallas_call(
        _ln_kernel,
        out_shape=(jax.ShapeDtypeStruct((m, d), F32), jax.ShapeDtypeStruct((m, d), BF16)),
        grid=(m // tm,),
        in_specs=[pl.BlockSpec((tm, d), lambda i: (i, 0)),
                  pl.BlockSpec((1, d), lambda i: (0, 0)),
                  pl.BlockSpec((1, d), lambda i: (0, 0))],
        out_specs=(pl.BlockSpec((tm, d), lambda i: (i, 0)),
                   pl.BlockSpec((tm, d), lambda i: (i, 0))),
        compiler_params=_params("parallel"),
        name="ln0",
    )(x, w.reshape(1, d), b.reshape(1, d))


def _proj_kernel(a_ref, w_ref, *rest, shift):
    o_ref = rest[-1]
    w = w_ref[...]
    if shift:
        w = jnp.concatenate([w[shift:], rest[0][...]], axis=0)
    o_ref[...] = lax.dot_general(a_ref[...], w.astype(BF16), NT_DIMS,
                                 preferred_element_type=F32).astype(o_ref.dtype)


def _in_proj(a, w_t, layer, row0, n, out_dtype, tm, tn, name):
    m, k = a.shape
    shift = row0 % tn
    base = row0 - shift
    in_specs = [pl.BlockSpec((tm, k), lambda i, j: (i, 0)),
                pl.BlockSpec((None, tn, k), lambda i, j: (layer, base // tn + j, 0))]
    if shift:
        in_specs.append(pl.BlockSpec((None, shift, k),
                                     lambda i, j: (layer, (base + (j + 1) * tn) // shift, 0)))
    return pl.pallas_call(
        functools.partial(_proj_kernel, shift=shift),
        out_shape=jax.ShapeDtypeStruct((m, n), out_dtype),
        grid=(m // tm, n // tn),
        in_specs=in_specs,
        out_specs=pl.BlockSpec((tm, tn), lambda i, j: (i, j)),
        compiler_params=_params("parallel", "arbitrary"),
        name=name,
    )(*([a] + [w_t] * (len(in_specs) - 1)))


def _mlstm_kernel(q_ref, k_ref, v_ref, og_ref, x_ref, wg_ref, gb_ref, cw_ref, cb_ref, nw_ref, *rest, n_cast):
    out_ref = rest[n_cast]
    qpad, kpad, c_s, n_s, m_s = rest[2 * n_cast + 1:]
    _cast_slabs(rest[:n_cast], rest[n_cast + 1:2 * n_cast + 1])
    blk = pl.program_id(1)
    tb = q_ref.shape[0]

    @pl.when(blk == 0)
    def _():
        qpad[0:8, :] = jnp.zeros((8, M_W), F32)
        kpad[0:8, :] = jnp.zeros((8, M_W), F32)
        c_s[...] = jnp.zeros_like(c_s)
        n_s[...] = jnp.zeros_like(n_s)
        m_s[...] = jnp.zeros_like(m_s)

    @pl.when(blk > 0)
    def _():
        qpad[0:8, :] = qpad[tb:tb + 8, :]
        kpad[0:8, :] = kpad[tb:tb + 8, :]

    qpad[8:8 + tb, :] = q_ref[...].astype(F32)
    kpad[8:8 + tb, :] = k_ref[...].astype(F32)

    def conv_silu(pad, cols, wofs):
        wcols = slice(wofs + cols.start, wofs + cols.stop)
        y = cb_ref[:, wcols]
        for j in range(CONV_K):
            off = 8 - (CONV_K - 1) + j
            y = y + cw_ref[j:j + 1, wcols] * pad[off:off + tb, cols]
        return y * _gate_sigmoid(y)

    ri = lax.broadcasted_iota(jnp.int32, (tb, tb), 0)
    ci = lax.broadcasted_iota(jnp.int32, (tb, tb), 1)
    causal = ri >= ci

    wg = wg_ref[...]
    wg = jnp.concatenate([wg, jnp.zeros((LANES - wg.shape[0], wg.shape[1]), wg.dtype)], axis=0)
    gb = lax.dot_general(x_ref[...], wg.astype(BF16), NT_DIMS, preferred_element_type=F32) + gb_ref[...]
    log_f = jnp.minimum(gb, 0.0) - jnp.log(1.0 + jnp.exp(-jnp.abs(gb)))
    bcum = _chunk_cumsum(log_f, tb)
    lane = lax.broadcasted_iota(jnp.int32, gb.shape, 1)
    pc = jnp.where(lane < M_HEADS, gb, bcum)
    pt = pc.T

    for h in range(M_HEADS):
        cols = slice(h * M_DH, (h + 1) * M_DH)
        qc = conv_silu(qpad, cols, 0)
        kc = conv_silu(kpad, cols, M_W) * (M_DH ** -0.5)
        vc = v_ref[:, cols]
        qb = qc.astype(BF16)

        i_col, b_col = pc[:, h:h + 1], pc[:, M_HEADS + h:M_HEADS + h + 1]
        i_row, b_row = pt[h:h + 1, :], pt[M_HEADS + h:M_HEADS + h + 1, :]
        m_prev = m_s[h]
        c_prev = c_s[h]
        n_prev = n_s[h]

        log_d = jnp.where(causal, b_col + (i_row - b_row), NEG_BIG)
        log_inter = b_col + m_prev
        m_t = jnp.maximum(log_inter, jnp.max(log_d, -1, keepdims=True))
        w_inter = jnp.exp(log_inter - m_t)
        s = lax.dot_general(qb, kc.astype(BF16), NT_DIMS, preferred_element_type=F32)
        s = s * jnp.exp(log_d - m_t)
        num = (jnp.dot(s.astype(BF16), vc, preferred_element_type=F32)
               + w_inter * jnp.dot(qb, c_prev.astype(BF16), preferred_element_type=F32))
        den = jnp.sum(s, -1, keepdims=True) + w_inter * jnp.sum(qc * n_prev, -1, keepdims=True)
        hh = num / jnp.maximum(jnp.abs(den), jnp.exp(-m_t))

        g = b_col[tb - 1:tb, :]
        log_w = g - b_col + i_col
        m_new = jnp.maximum(g + m_prev, jnp.max(log_w, 0, keepdims=True))
        decay = jnp.exp(g + m_prev - m_new)
        kw = kc * jnp.exp(log_w - m_new)
        c_s[h] = decay * c_prev + lax.dot_general(kw.astype(BF16), vc, TN_DIMS,
                                                  preferred_element_type=F32)
        n_s[h] = decay * n_prev + jnp.sum(kw, 0, keepdims=True)
        m_s[h] = m_new

        hn = _rms_norm(hh, nw_ref[:, cols]) * _gate_sigmoid(og_ref[:, cols].astype(F32))
        out_ref[:, cols] = hn.astype(out_ref.dtype)


def _mlstm(proj, x, w_t, gate_b, conv_w, conv_b, norm_w, layer, cast_weights, batch, seq, tb=M_CHUNK):
    nblk = seq // tb
    ng = 2 * M_HEADS
    col = lambda off: pl.BlockSpec((tb, M_W), lambda b, s: (b * nblk + s, off // M_W))
    full = lambda a: pl.BlockSpec(a.shape, lambda b, s: (0, 0))
    conv_b = conv_b.reshape(1, -1)
    norm_w = norm_w.reshape(1, -1)
    c_in, c_out, c_shapes = _cast_riders(cast_weights, layer, batch * nblk, lambda b, s: b * nblk + s)
    return pl.pallas_call(
        functools.partial(_mlstm_kernel, n_cast=len(cast_weights)),
        out_shape=[jax.ShapeDtypeStruct((batch * seq, M_W), BF16)] + c_shapes,
        grid=(batch, nblk),
        in_specs=[col(OFF_MQ), col(OFF_MK), col(OFF_MV), col(OFF_MO),
                  pl.BlockSpec((tb, x.shape[1]), lambda b, s: (b * nblk + s, 0)),
                  pl.BlockSpec((None, ng, w_t.shape[2]), lambda b, s: (layer, PROJ_M_W // ng, 0)),
                  full(gate_b), full(conv_w), full(conv_b), full(norm_w)] + c_in,
        out_specs=[pl.BlockSpec((tb, M_W), lambda b, s: (b * nblk + s, 0))] + c_out,
        scratch_shapes=[pltpu.VMEM((tb + 8, M_W), F32), pltpu.VMEM((tb + 8, M_W), F32),
                        pltpu.VMEM((M_HEADS, M_DH, M_DH), F32), pltpu.VMEM((M_HEADS, 1, M_DH), F32),
                        pltpu.VMEM((M_HEADS, 1, 1), F32)],
        compiler_params=_params("parallel", "arbitrary"),
        name="mlstm",
    )(proj, proj, proj, proj, x, w_t, gate_b, conv_w, conv_b, norm_w, *cast_weights)


def _rope_table_kernel(pos_ref, inv_ref, same_ref, lo_ref, hi_ref):
    ang = pos_ref[...].astype(F32) * inv_ref[...]
    lane = lax.broadcasted_iota(jnp.int32, ang.shape, 1) & (A_DH - 1)
    half = ROT_DIM // 2
    cos, sin = jnp.cos(ang), jnp.sin(ang)
    same_ref[...] = jnp.where(lane < ROT_DIM, cos, 1.0)
    lo_ref[...] = jnp.where((lane >= half) & (lane < ROT_DIM), sin, 0.0)
    hi_ref[...] = jnp.where(lane < half, -sin, 0.0)


def _rope_tables(pos, inv, tm=512):
    m = pos.shape[0]
    out = pl.BlockSpec((tm, LANES), lambda i: (i, 0))
    shp = jax.ShapeDtypeStruct((m, LANES), F32)
    return pl.pallas_call(
        _rope_table_kernel,
        out_shape=(shp, shp, shp),
        grid=(m // tm,),
        in_specs=[pl.BlockSpec((tm, 1), lambda i: (i, 0)), pl.BlockSpec((1, LANES), lambda i: (0, 0))],
        out_specs=(out, out, out),
        compiler_params=_params("parallel"),
        name="rope_tables",
    )(pos, inv)


def _rotate(x, same, lo, hi):
    half = ROT_DIM // 2
    x = x.astype(F32)
    return x * same + pltpu.roll(x, half, axis=1) * lo + pltpu.roll(x, LANES - half, axis=1) * hi


def _attn_kernel(lam_ref, q_ref, k_ref, v_ref, same_ref, lo_ref, hi_ref, nw_ref, o_ref,
                 krot, vext, s_buf, m_s, acc_s, *, lam_init, tq, strip, heads):
    qi = pl.program_id(2)
    hw = 2 * A_DH
    lv = lam_ref[...]
    lam = (jnp.exp(jnp.sum(lv[0:1] * lv[1:2], -1, keepdims=True))
           - jnp.exp(jnp.sum(lv[2:3] * lv[3:4], -1, keepdims=True)) + lam_init)
    head_lanes = [slice(g * hw, (g + 1) * hw) for g in range(heads)]

    @pl.when(qi == 0)
    def _():
        for g, hl in enumerate(head_lanes):
            krot[:, hl] = _rotate(k_ref[:, hl], same_ref[...], lo_ref[...], hi_ref[...]).astype(BF16)
            vext[g, :, :hw] = v_ref[:, hl]
            vext[g, :, hw:] = jnp.ones((vext.shape[1], hw), BF16)

    q0 = pl.multiple_of(qi * tq, tq)
    tabs = [t[pl.ds(q0, tq), :] for t in (same_ref, lo_ref, hi_ref)]
    lane = lax.broadcasted_iota(jnp.int32, (tq, hw), 1)
    qq = []
    for hl in head_lanes:
        q = (_rotate(q_ref[:, hl], *tabs) * (A_DH ** -0.5 * LOG2_E)).astype(BF16)
        zero = jnp.zeros_like(q)
        qq.append(jnp.concatenate([jnp.where(lane < A_DH, q, zero), jnp.where(lane >= A_DH, q, zero)],
                                  axis=0))
    m_s[...] = jnp.full(m_s.shape, NEG_BIG, F32)
    acc_s[...] = jnp.zeros_like(acc_s)

    strips = [slice(r, r + strip) for r in range(0, 2 * tq, strip)]

    def scores(g, j, rows):
        r0 = pl.multiple_of(j * tq, tq)
        return lax.dot_general(qq[g][rows], krot[pl.ds(r0, tq), head_lanes[g]], NT_DIMS,
                               preferred_element_type=F32)

    def block(j, diagonal):
        r0 = pl.multiple_of(j * tq, tq)
        for rows in strips:
            for g in range(heads):
                if diagonal:
                    nk = (rows.start & (tq - 1)) + strip
                    s = s_buf[g, rows, :nk]
                    row = (lax.broadcasted_iota(jnp.int32, s.shape, 0) + rows.start) & (tq - 1)
                    col = lax.broadcasted_iota(jnp.int32, s.shape, 1)
                    s = jnp.where(col <= row, s, NEG_BIG)
                else:
                    nk = tq
                    s = s_buf[g, rows, :]
                    s_buf[g, rows, :] = scores(g, j + 1, rows)
                m_prev = m_s[g, rows, :]
                m_new = jnp.maximum(m_prev, jnp.max(s, -1, keepdims=True))
                p = jnp.exp2(s - jnp.concatenate([m_new] * (nk // LANES), axis=1))
                a = jnp.exp2(m_prev - m_new)
                acc_s[g, rows, :] = (jnp.concatenate([a, a], axis=1) * acc_s[g, rows, :]
                                     + jnp.dot(p.astype(BF16), vext[g, pl.ds(r0, nk), :],
                                               preferred_element_type=F32))
                m_s[g, rows, :] = m_new

    def off_diagonal(j, carry):
        block(j, False)
        return carry

    for rows in strips:
        for g in range(heads):
            s_buf[g, rows, :] = scores(g, 0, rows)
    lax.fori_loop(0, qi, off_diagonal, 0)
    block(qi, True)

    for g, hl in enumerate(head_lanes):
        acc = acc_s[g]
        o = acc[:, :hw] / acc[:, hw:]
        o = o[:tq] - lam * o[tq:]
        o_ref[:, hl] = (_rms_norm(o, nw_ref[:, hl]) * (1.0 - lam_init)).astype(o_ref.dtype)


def _attention(proj, tables, lam_vecs, norm_w, lam_init, batch, seq, tq=1024, strip=128, heads=2):
    nq = seq // tq
    hw = 2 * A_DH
    gw = heads * hw
    kv = lambda off: pl.BlockSpec((seq, gw), lambda b, h, i: (b, off // gw + h))
    tab = pl.BlockSpec((seq, LANES), lambda b, h, i: (b, 0))
    return pl.pallas_call(
        functools.partial(_attn_kernel, lam_init=lam_init, tq=tq, strip=min(strip, 2 * tq), heads=heads),
        out_shape=jax.ShapeDtypeStruct((batch * seq, A_W), BF16),
        grid=(batch, A_HEADS // heads, nq),
        in_specs=[pl.BlockSpec((4, A_DH), lambda b, h, i: (0, 0)),
                  pl.BlockSpec((tq, gw), lambda b, h, i: (b * nq + i, OFF_AQ // gw + h)),
                  kv(OFF_AK), kv(OFF_AV), tab, tab, tab,
                  pl.BlockSpec((1, gw), lambda b, h, i: (0, h))],
        out_specs=pl.BlockSpec((tq, gw), lambda b, h, i: (b * nq + i, h)),
        scratch_shapes=[pltpu.VMEM((seq, gw), BF16), pltpu.VMEM((heads, seq, 2 * hw), BF16),
                        pltpu.VMEM((heads, 2 * tq, tq), F32), pltpu.VMEM((heads, 2 * tq, LANES), F32),
                        pltpu.VMEM((heads, 2 * tq, 2 * hw), F32)],
        compiler_params=_params("parallel", "parallel", "arbitrary"),
        name="diff_attn",
    )(lam_vecs, proj, proj, proj, *tables, norm_w.reshape(1, -1))


def _hgrn_kernel(lbl_ref, q_ref, f_ref, v_ref, og_ref, nw_ref, *rest, layer, heads, n_cast):
    out_ref = rest[n_cast]
    qs, ks, bs, st_s = rest[2 * n_cast + 1:]
    _cast_slabs(rest[:n_cast], rest[n_cast + 1:2 * n_cast + 1])
    blk = pl.program_id(2)
    tb = q_ref.shape[0]
    nchunk = tb // CHUNK
    mid = CHUNK // 2 - 1
    head_lanes = [slice(g * G_DK, (g + 1) * G_DK) for g in range(heads)]

    @pl.when(blk == 0)
    def _():
        st_s[...] = jnp.zeros_like(st_s)

    logits = lbl_ref[...]
    e = jnp.exp(logits - jnp.max(logits, 0, keepdims=True))
    p = e / jnp.sum(e, 0, keepdims=True)
    lb = jnp.sum(p[0:layer + 1], 0, keepdims=True) - p[0:1]

    f_g = lb + (1.0 - lb) * _sigmoid(f_ref[...].astype(F32))
    bs[...] = _chunk_cumsum(jnp.log(jnp.maximum(f_g, F_FLOOR)), CHUNK)
    ks[...] = 1.0 - f_g
    qv = q_ref[...].astype(F32)
    qs[...] = qv * _gate_sigmoid(qv)

    def finish(r0, hl, o):
        gg = og_ref[pl.ds(r0, CHUNK), hl].astype(F32)
        hn = _rms_norm(o, nw_ref[:, hl]) * (gg * _gate_sigmoid(gg))
        out_ref[pl.ds(r0, CHUNK), hl] = hn.astype(out_ref.dtype)

    def state_terms(bc, qc, kc, vb, st):
        b_last = bc[CHUNK - 1:CHUNK]
        o_inter = lax.dot_general((qc * jnp.exp(bc)).astype(BF16), st.astype(BF16), NT_DIMS,
                                  preferred_element_type=F32)
        k_dec = (kc * jnp.exp(b_last - bc)).astype(BF16)
        st_new = st * jnp.exp(b_last) + lax.dot_general(vb, k_dec, TN_DIMS, preferred_element_type=F32)
        return o_inter, st_new

    spread = jnp.zeros((1, bs.shape[1]), F32)
    for c in range(nchunk):
        b_mid = bs[c * CHUNK + mid:c * CHUNK + mid + 1, :]
        spread = jnp.maximum(spread, jnp.maximum(bs[c * CHUNK:c * CHUNK + 1, :] - b_mid,
                                                 b_mid - bs[(c + 1) * CHUNK - 1:(c + 1) * CHUNK, :]))
    narrow = jnp.max(spread) <= FACTOR_SPREAD

    @pl.when(narrow)
    def _():
        ri = lax.broadcasted_iota(jnp.int32, (CHUNK, CHUNK), 0)
        ci = lax.broadcasted_iota(jnp.int32, (CHUNK, CHUNK), 1)
        causal = ri >= ci
        st = [st_s[g] for g in range(heads)]
        for c in range(nchunk):
            r0 = c * CHUNK
            for g, hl in enumerate(head_lanes):
                bc, qc, kc = bs[r0:r0 + CHUNK, hl], qs[r0:r0 + CHUNK, hl], ks[r0:r0 + CHUNK, hl]
                vb = v_ref[r0:r0 + CHUNK, hl]
                b_mid = bc[mid:mid + 1]
                q_sc = (qc * jnp.exp(bc - b_mid)).astype(BF16)
                k_sc = (kc * jnp.exp(b_mid - bc)).astype(BF16)
                a = lax.dot_general(q_sc, k_sc, NT_DIMS, preferred_element_type=F32)
                a = jnp.where(causal, a, 0.0).astype(BF16)
                o_inter, st[g] = state_terms(bc, qc, kc, vb, st[g])
                finish(r0, hl, o_inter + jnp.dot(a, vb, preferred_element_type=F32))
        for g in range(heads):
            st_s[g] = st[g]

    @pl.when(jnp.logical_not(narrow))
    def _():
        row_c = lax.broadcasted_iota(jnp.int32, (CHUNK, G_DK), 0)
        row_s = lax.broadcasted_iota(jnp.int32, (SUB, G_DK), 0)

        def chunk_step(c, carry):
            r0 = pl.multiple_of(c * CHUNK, CHUNK)
            for g, hl in enumerate(head_lanes):
                bc = bs[pl.ds(r0, CHUNK), hl]
                qc = qs[pl.ds(r0, CHUNK), hl]
                kc = ks[pl.ds(r0, CHUNK), hl]
                vb = v_ref[pl.ds(r0, CHUNK), hl]
                vc = vb.astype(F32)
                o_inter, st_new = state_terms(bc, qc, kc, vb, st_s[g])
                outs = []
                for i in range(CHUNK // SUB):
                    r = slice(i * SUB, (i + 1) * SUB)
                    b_i, q_i, k_i, v_i = bc[r], qc[r], kc[r], vc[r]
                    o_i = o_inter[r]
                    if i > 0:
                        c_i = b_i[0:1]
                        q_sc = (q_i * jnp.exp(b_i - c_i)).astype(BF16)
                        k_sc = jnp.where(row_c < i * SUB, kc * jnp.exp(jnp.minimum(c_i - bc, 0.0)), 0.0)
                        a_off = lax.dot_general(q_sc, k_sc.astype(BF16), NT_DIMS,
                                                preferred_element_type=F32)
                        o_i = o_i + jnp.dot(a_off.astype(BF16), vb, preferred_element_type=F32)
                    for s in range(SUB):
                        dec = jnp.exp(jnp.minimum(b_i - b_i[s:s + 1], 0.0))
                        a_s = jnp.sum(jnp.where(row_s >= s, q_i * (k_i[s:s + 1] * dec), 0.0), -1,
                                      keepdims=True)
                        o_i = o_i + a_s * v_i[s:s + 1]
                    outs.append(o_i)
                st_s[g] = st_new
                finish(r0, hl, jnp.concatenate(outs, axis=0))
            return carry

        lax.fori_loop(0, nchunk, chunk_step, 0)


def _hgrn(proj, lb_logits, norm_w, layer, cast_weights, batch, seq, tb=1024, heads=4):
    nblk = seq // tb
    nh = G_HEADS // heads
    w = heads * G_DK
    col = lambda off: (lambda b, h, s: (b * nblk + s, off // w + h))
    vec = pl.BlockSpec((1, w), lambda b, h, s: (0, h))
    c_in, c_out, c_shapes = _cast_riders(cast_weights, layer, batch * nh * nblk,
                                         lambda b, h, s: (b * nh + h) * nblk + s)
    return pl.pallas_call(
        functools.partial(_hgrn_kernel, layer=layer, heads=heads, n_cast=len(cast_weights)),
        out_shape=[jax.ShapeDtypeStruct((batch * seq, G_WV), BF16)] + c_shapes,
        grid=(batch, nh, nblk),
        in_specs=[pl.BlockSpec((DEPTH, w), lambda b, h, s: (0, h)),
                  pl.BlockSpec((tb, w), col(OFF_GQ)),
                  pl.BlockSpec((tb, w), col(OFF_GF)),
                  pl.BlockSpec((tb, w), col(OFF_GI)),
                  pl.BlockSpec((tb, w), col(OFF_GG)),
                  vec] + c_in,
        out_specs=[pl.BlockSpec((tb, w), lambda b, h, s: (b * nblk + s, h))] + c_out,
        scratch_shapes=[pltpu.VMEM((tb, w), F32), pltpu.VMEM((tb, w), F32), pltpu.VMEM((tb, w), F32),
                        pltpu.VMEM((heads, G_DK, G_DK), F32)],
        compiler_params=_params("parallel", "parallel", "arbitrary"),
        name="hgrn2",
    )(lb_logits, proj, proj, proj, proj, norm_w.reshape(1, -1), *cast_weights)


def _mix_kernel(hm_ref, ha_ref, hg_ref, g0_ref, g1_ref, g2_ref, pm_ref, pa_ref, pg_ref, wo_ref,
                h_ref, lw_ref, lb_ref, o_ref, obf_ref):
    def branch(x_ref, p_ref, g_ref):
        return (_gate_sigmoid(g_ref[...].astype(F32))
                * jnp.dot(x_ref[...], p_ref[...], preferred_element_type=F32))

    y = branch(hm_ref, pm_ref, g0_ref) + branch(ha_ref, pa_ref, g1_ref) + branch(hg_ref, pg_ref, g2_ref)
    mix = jnp.dot(y.astype(BF16), wo_ref[...], preferred_element_type=F32)
    h = _layer_norm(ALPHA * h_ref[...] + mix, lw_ref[...], lb_ref[...])
    o_ref[...] = h
    obf_ref[...] = h.astype(BF16)


def _mix_ln(hm, ha, hg, gate_pre, p_m, p_a, p_g, w_out, h, ln_w, ln_b, tm=256):
    m, d = h.shape
    row = pl.BlockSpec((tm, d), lambda i: (i, 0))
    vec = pl.BlockSpec((1, d), lambda i: (0, 0))
    act = lambda w: pl.BlockSpec((tm, w), lambda i: (i, 0))
    gate = lambda br: pl.BlockSpec((tm, d), lambda i: (i, br))
    wgt = lambda w: pl.BlockSpec(w.shape, lambda i: (0, 0), pipeline_mode=pl.Buffered(1))
    return pl.pallas_call(
        _mix_kernel,
        out_shape=(jax.ShapeDtypeStruct((m, d), F32), jax.ShapeDtypeStruct((m, d), BF16)),
        grid=(m // tm,),
        in_specs=[act(M_W), act(A_W), act(G_WV), gate(0), gate(1), gate(2),
                  wgt(p_m), wgt(p_a), wgt(p_g), wgt(w_out), row, vec, vec],
        out_specs=(row, row),
        compiler_params=_params("parallel"),
        name="mix_ln",
    )(hm, ha, hg, gate_pre, gate_pre, gate_pre, p_m, p_a, p_g, w_out, h,
      ln_w.reshape(1, d), ln_b.reshape(1, d))


def _mlp_kernel(hbf_ref, wu_ref, wd_ref, h_ref, lw_ref, lb_ref, o_ref, obf_ref):
    j = pl.program_id(1)

    @pl.when(j == 0)
    def _():
        o_ref[...] = jnp.zeros_like(o_ref)

    up = jnp.maximum(jnp.dot(hbf_ref[...], wu_ref[...], preferred_element_type=F32), 0.0)
    o_ref[...] += jnp.dot((up * up).astype(BF16), wd_ref[...], preferred_element_type=F32)

    @pl.when(j == pl.num_programs(1) - 1)
    def _():
        h = _layer_norm(ALPHA * h_ref[...] + o_ref[...], lw_ref[...], lb_ref[...])
        o_ref[...] = h
        obf_ref[...] = h.astype(BF16)


def _mlp_ln(hbf, w_up, w_down, h, ln_w, ln_b, tm=512, tf=1024):
    m, d = h.shape
    f = w_up.shape[1]
    row = pl.BlockSpec((tm, d), lambda i, j: (i, 0))
    vec = pl.BlockSpec((1, d), lambda i, j: (0, 0))
    return pl.pallas_call(
        _mlp_kernel,
        out_shape=(jax.ShapeDtypeStruct((m, d), F32), jax.ShapeDtypeStruct((m, d), BF16)),
        grid=(m // tm, f // tf),
        in_specs=[row, pl.BlockSpec((d, tf), lambda i, j: (0, j)),
                  pl.BlockSpec((tf, d), lambda i, j: (j, 0)), row, vec, vec],
        out_specs=(row, row),
        compiler_params=_params("parallel", "arbitrary"),
        name="mlp_ln",
    )(hbf, w_up, w_down, h, ln_w.reshape(1, d), ln_b.reshape(1, d))


def kernel(x, positions, ln0_w, ln0_b, w_in, m_conv_w, m_conv_b, m_gate_b, m_norm_w, a_lambda, a_norm_w,
           g_lb_logits, g_norm_w, p_m, p_a, p_g, w_out, ln1_w, ln1_b, w_up, w_down, ln2_w, ln2_b):
    batch, seq, d = x.shape
    m = batch * seq
    pos = positions.reshape(m, 1)
    inv = jnp.power(jnp.float32(ROPE_THETA), -jnp.arange(0, ROT_DIM, 2, dtype=F32) / ROT_DIM)
    inv = jnp.tile(inv, LANES // inv.shape[0]).reshape(1, LANES)

    w_t = jnp.swapaxes(w_in, 1, 2)
    row_r = PROJ_M_W + 2 * M_HEADS
    gate_b = jnp.pad(m_gate_b, ((0, 0), (0, LANES - 2 * M_HEADS)))

    tables = _rope_tables(pos, inv)
    h, hbf = _ln(x.reshape(m, d), ln0_w, ln0_b)
    for l in range(DEPTH):
        proj_m = _in_proj(hbf, w_t, l, 0, PROJ_M_W, BF16, 2048, 1024, "in_proj_m")
        proj_r = _in_proj(hbf, w_t, l, row_r, OFF_GATE, BF16, 2048, 1024, "in_proj_r")
        proj_g = _in_proj(hbf, w_t, l, row_r + OFF_GATE, N_BRANCH * D_MODEL, BF16, 2048, 1024, "in_proj_g")

        hm, p_m_l, p_a_l, p_g_l, w_out_l = _mlstm(proj_m, hbf, w_t, gate_b[l:l + 1], m_conv_w[l], m_conv_b[l],
                                                  m_norm_w[l], l, (p_m, p_a, p_g, w_out), batch, seq)
        lam_init = 0.8 - 0.6 * math.exp(-0.3 * l)
        ha = _attention(proj_r, tables, a_lambda[l], a_norm_w[l], lam_init, batch, seq)
        hg, w_up_l, w_down_l = _hgrn(proj_r, g_lb_logits, g_norm_w[l], l, (w_up, w_down), batch, seq)

        h, hbf = _mix_ln(hm, ha, hg, proj_g, p_m_l, p_a_l, p_g_l, w_out_l, h, ln1_w[l], ln1_b[l])
        h, hbf = _mlp_ln(hbf, w_up_l, w_down_l, h, ln2_w[l], ln2_b[l])
    return h.reshape(batch, seq, d)
```

```python
import functools
import math

import jax
import jax.numpy as jnp
from jax import lax
from jax.experimental import pallas as pl
from jax.experimental.pallas import tpu as pltpu

F32 = jnp.float32
BF16 = jnp.bfloat16

D_MODEL = 2048
DEPTH = 2
M_HEADS = 4
M_DH = 256
M_W = M_HEADS * M_DH
CONV_K = 4
A_HEADS = 8
A_DH = 64
A_W = A_HEADS * 2 * A_DH
ROT_DIM = A_DH // 4
ROPE_THETA = 500000.0
G_HEADS = 8
G_DK = 128
G_WK = G_HEADS * G_DK
G_WV = G_HEADS * G_DK
CHUNK = 64
M_CHUNK = 256
SUB = 16
FACTOR_SPREAD = 60.0
N_BRANCH = 3
D_FF = 4 * D_MODEL
ALPHA = (2 * DEPTH) ** 0.25
LN_EPS = 1e-5
NORM_EPS = 1e-6
NEG_BIG = -1e30
LOG2_E = math.log2(math.e)
F_FLOOR = 1e-30

OFF_MQ, OFF_MK, OFF_MV, OFF_MO = 0, M_W, 2 * M_W, 3 * M_W
PROJ_M_W = 4 * M_W
OFF_AQ = 0
OFF_AK = OFF_AQ + A_W
OFF_AV = OFF_AK + A_W
OFF_GQ = OFF_AV + A_W
OFF_GF = OFF_GQ + G_WK
OFF_GI = OFF_GF + G_WK
OFF_GG = OFF_GI + G_WV
OFF_GATE = OFF_GG + G_WV
PROJ_R_W = OFF_GATE + N_BRANCH * D_MODEL
LANES = 128
VMEM_LIMIT = 56 * 1024 * 1024

NT_DIMS = (((1,), (1,)), ((), ()))
TN_DIMS = (((0,), (0,)), ((), ()))


def _params(*sem):
    return pltpu.CompilerParams(dimension_semantics=sem, vmem_limit_bytes=VMEM_LIMIT)


def _sigmoid(x):
    return 1.0 / (1.0 + jnp.exp(-x))


def _gate_sigmoid(x):
    return 0.5 * jnp.tanh(0.5 * x) + 0.5


def _layer_norm(x, w, b):
    mu = jnp.mean(x, -1, keepdims=True)
    xc = x - mu
    var = jnp.mean(xc * xc, -1, keepdims=True)
    return xc * lax.rsqrt(var + LN_EPS) * w + b


def _rms_norm(x, w):
    return x * lax.rsqrt(jnp.mean(x * x, -1, keepdims=True) + NORM_EPS) * w


def _chunk_cumsum(x, chunk):
    row = lax.broadcasted_iota(jnp.int32, x.shape, 0) & (chunk - 1)
    sh = 1
    while sh < chunk:
        x = x + jnp.where(row >= sh, pltpu.roll(x, sh, axis=0), 0.0)
        sh *= 2
    return x


def _cast_riders(weights, layer, steps, step_index):
    ins = [pl.BlockSpec((None, w.shape[1] // steps, w.shape[2]), lambda *g: (layer, step_index(*g), 0))
           for w in weights]
    outs = [pl.BlockSpec((w.shape[1] // steps, w.shape[2]), lambda *g: (step_index(*g), 0)) for w in weights]
    shapes = [jax.ShapeDtypeStruct(w.shape[1:], BF16) for w in weights]
    return ins, outs, shapes


def _cast_slabs(srcs, dsts):
    for src, dst in zip(srcs, dsts):
        dst[...] = src[...].astype(dst.dtype)


def _ln_kernel(x_ref, w_ref, b_ref, o_ref, obf_ref):
    y = _layer_norm(x_ref[...], w_ref[...], b_ref[...])
    o_ref[...] = y
    obf_ref[...] = y.astype(BF16)


def _ln(x, w, b, tm=256):
    m, d = x.shape
    return pl.pallas_call(
        _ln_kernel,
        out_shape=(jax.ShapeDtypeStruct((m, d), F32), jax.ShapeDtypeStruct((m, d), BF16)),
        grid=(m // tm,),
        in_specs=[pl.BlockSpec((tm, d), lambda i: (i, 0)),
                  pl.BlockSpec((1, d), lambda i: (0, 0)),
                  pl.BlockSpec((1, d), lambda i: (0, 0))],
        out_specs=(pl.BlockSpec((tm, d), lambda i: (i, 0)),
                   pl.BlockSpec((tm, d), lambda i: (i, 0))),
        compiler_params=_params("parallel"),
        name="ln0",
    )(x, w.reshape(1, d), b.reshape(1, d))


def _proj_kernel(a_ref, w_ref, *rest, shift):
    o_ref = rest[-1]
    w = w_ref[...]
    if shift:
        w = jnp.concatenate([w[shift:], rest[0][...]], axis=0)
    o_ref[...] = lax.dot_general(a_ref[...], w.astype(BF16), NT_DIMS,
                                 preferred_element_type=F32).astype(o_ref.dtype)


def _in_proj(a, w_t, layer, row0, n, out_dtype, tm, tn, name):
    m, k = a.shape
    shift = row0 % tn
    base = row0 - shift
    in_specs = [pl.BlockSpec((tm, k), lambda i, j: (i, 0)),
                pl.BlockSpec((None, tn, k), lambda i, j: (layer, base // tn + j, 0))]
    if shift:
        in_specs.append(pl.BlockSpec((None, shift, k),
                                     lambda i, j: (layer, (base + (j + 1) * tn) // shift, 0)))
    return pl.pallas_call(
        functools.partial(_proj_kernel, shift=shift),
        out_shape=jax.ShapeDtypeStruct((m, n), out_dtype),
        grid=(m // tm, n // tn),
        in_specs=in_specs,
        out_specs=pl.BlockSpec((tm, tn), lambda i, j: (i, j)),
        compiler_params=_params("parallel", "arbitrary"),
        name=name,
    )(*([a] + [w_t] * (len(in_specs) - 1)))


def _mlstm_kernel(q_ref, k_ref, v_ref, og_ref, x_ref, wg_ref, gb_ref, cw_ref, cb_ref, nw_ref, *rest, n_cast):
    out_ref = rest[n_cast]
    qtail, ktail, c_s, n_s, m_s = rest[2 * n_cast + 1:]
    _cast_slabs(rest[:n_cast], rest[n_cast + 1:2 * n_cast + 1])
    blk = pl.program_id(1)
    tb = q_ref.shape[0]

    @pl.when(blk == 0)
    def _():
        qtail[...] = jnp.zeros_like(qtail)
        ktail[...] = jnp.zeros_like(ktail)
        c_s[...] = jnp.zeros_like(c_s)
        n_s[...] = jnp.zeros_like(n_s)
        m_s[...] = jnp.zeros_like(m_s)

    ri = lax.broadcasted_iota(jnp.int32, (tb, tb), 0)
    ci = lax.broadcasted_iota(jnp.int32, (tb, tb), 1)
    causal = ri >= ci
    shifts = [jnp.where(ri - ci == d, 1.0, 0.0).astype(BF16) for d in range(1, CONV_K)]
    row8 = lax.broadcasted_iota(jnp.int32, (8, M_W), 0)

    def conv_taps(x_ref, tail):
        x = x_ref[...]
        t8 = tail[...]
        taps = [x.astype(F32)]
        for d, s in zip(range(1, CONV_K), shifts):
            xd = jnp.dot(s, x, preferred_element_type=F32)
            top = xd[0:8] + jnp.where(row8 < d, pltpu.roll(t8, d, axis=0), 0.0)
            taps.append(jnp.concatenate([top, xd[8:]], axis=0))
        tail[...] = taps[0][tb - 8:tb]
        return taps

    def conv_silu(taps, cols, wofs):
        wcols = slice(wofs + cols.start, wofs + cols.stop)
        y = cb_ref[:, wcols]
        for j in range(CONV_K):
            y = y + cw_ref[j:j + 1, wcols] * taps[CONV_K - 1 - j][:, cols]
        return y * _gate_sigmoid(y)

    q_taps = conv_taps(q_ref, qtail)
    k_taps = conv_taps(k_ref, ktail)

    wg = wg_ref[...]
    wg = jnp.concatenate([wg, jnp.zeros((LANES - wg.shape[0], wg.shape[1]), wg.dtype)], axis=0)
    gb = lax.dot_general(x_ref[...], wg.astype(BF16), NT_DIMS, preferred_element_type=F32) + gb_ref[...]
    log_f = jnp.minimum(gb, 0.0) - jnp.log(1.0 + jnp.exp(-jnp.abs(gb)))
    bcum = _chunk_cumsum(log_f, tb)
    lane = lax.broadcasted_iota(jnp.int32, gb.shape, 1)
    pc = jnp.where(lane < M_HEADS, gb, bcum)
    pt = pc.T

    for h in range(M_HEADS):
        cols = slice(h * M_DH, (h + 1) * M_DH)
        qc = conv_silu(q_taps, cols, 0)
        kc = conv_silu(k_taps, cols, M_W) * (M_DH ** -0.5)
        vc = v_ref[:, cols]
        qb = qc.astype(BF16)

        i_col, b_col = pc[:, h:h + 1], pc[:, M_HEADS + h:M_HEADS + h + 1]
        i_row, b_row = pt[h:h + 1, :], pt[M_HEADS + h:M_HEADS + h + 1, :]
        m_prev = m_s[h]
        c_prev = c_s[h]
        n_prev = n_s[h]

        log_d = jnp.where(causal, b_col + (i_row - b_row), NEG_BIG)
        log_inter = b_col + m_prev
        m_t = jnp.maximum(log_inter, jnp.max(log_d, -1, keepdims=True))
        w_inter = jnp.exp(log_inter - m_t)
        s = lax.dot_general(qb, kc.astype(BF16), NT_DIMS, preferred_element_type=F32)
        s = s * jnp.exp(log_d - m_t)
        num = (jnp.dot(s.astype(BF16), vc, preferred_element_type=F32)
               + w_inter * jnp.dot(qb, c_prev.astype(BF16), preferred_element_type=F32))
        den = jnp.sum(s, -1, keepdims=True) + w_inter * jnp.sum(qc * n_prev, -1, keepdims=True)
        hh = num / jnp.maximum(jnp.abs(den), jnp.exp(-m_t))

        g = b_col[tb - 1:tb, :]
        log_w = g - b_col + i_col
        m_new = jnp.maximum(g + m_prev, jnp.max(log_w, 0, keepdims=True))
        decay = jnp.exp(g + m_prev - m_new)
        kw = kc * jnp.exp(log_w - m_new)
        c_s[h] = decay * c_prev + lax.dot_general(kw.astype(BF16), vc, TN_DIMS,
                                                  preferred_element_type=F32)
        n_s[h] = decay * n_prev + jnp.sum(kw, 0, keepdims=True)
        m_s[h] = m_new

        hn = _rms_norm(hh, nw_ref[:, cols]) * _gate_sigmoid(og_ref[:, cols].astype(F32))
        out_ref[:, cols] = hn.astype(out_ref.dtype)


def _mlstm(proj, x, w_t, gate_b, conv_w, conv_b, norm_w, layer, cast_weights, batch, seq, tb=M_CHUNK):
    nblk = seq // tb
    ng = 2 * M_HEADS
    col = lambda off: pl.BlockSpec((tb, M_W), lambda b, s: (b * nblk + s, off // M_W))
    full = lambda a: pl.BlockSpec(a.shape, lambda b, s: (0, 0))
    conv_b = conv_b.reshape(1, -1)
    norm_w = norm_w.reshape(1, -1)
    c_in, c_out, c_shapes = _cast_riders(cast_weights, layer, batch * nblk, lambda b, s: b * nblk + s)
    return pl.pallas_call(
        functools.partial(_mlstm_kernel, n_cast=len(cast_weights)),
        out_shape=[jax.ShapeDtypeStruct((batch * seq, M_W), BF16)] + c_shapes,
        grid=(batch, nblk),
        in_specs=[col(OFF_MQ), col(OFF_MK), col(OFF_MV), col(OFF_MO),
                  pl.BlockSpec((tb, x.shape[1]), lambda b, s: (b * nblk + s, 0)),
                  pl.BlockSpec((None, ng, w_t.shape[2]), lambda b, s: (layer, PROJ_M_W // ng, 0)),
                  full(gate_b), full(conv_w), full(conv_b), full(norm_w)] + c_in,
        out_specs=[pl.BlockSpec((tb, M_W), lambda b, s: (b * nblk + s, 0))] + c_out,
        scratch_shapes=[pltpu.VMEM((8, M_W), F32), pltpu.VMEM((8, M_W), F32),
                        pltpu.VMEM((M_HEADS, M_DH, M_DH), F32), pltpu.VMEM((M_HEADS, 1, M_DH), F32),
                        pltpu.VMEM((M_HEADS, 1, 1), F32)],
        compiler_params=_params("parallel", "arbitrary"),
        name="mlstm",
    )(proj, proj, proj, proj, x, w_t, gate_b, conv_w, conv_b, norm_w, *cast_weights)


def _rope_table_kernel(pos_ref, inv_ref, same_ref, lo_ref, hi_ref):
    ang = pos_ref[...].astype(F32) * inv_ref[...]
    lane = lax.broadcasted_iota(jnp.int32, ang.shape, 1) & (A_DH - 1)
    half = ROT_DIM // 2
    cos, sin = jnp.cos(ang), jnp.sin(ang)
    same_ref[...] = jnp.where(lane < ROT_DIM, cos, 1.0)
    lo_ref[...] = jnp.where((lane >= half) & (lane < ROT_DIM), sin, 0.0)
    hi_ref[...] = jnp.where(lane < half, -sin, 0.0)


def _rope_tables(pos, inv, tm=512):
    m = pos.shape[0]
    out = pl.BlockSpec((tm, LANES), lambda i: (i, 0))
    shp = jax.ShapeDtypeStruct((m, LANES), F32)
    return pl.pallas_call(
        _rope_table_kernel,
        out_shape=(shp, shp, shp),
        grid=(m // tm,),
        in_specs=[pl.BlockSpec((tm, 1), lambda i: (i, 0)), pl.BlockSpec((1, LANES), lambda i: (0, 0))],
        out_specs=(out, out, out),
        compiler_params=_params("parallel"),
        name="rope_tables",
    )(pos, inv)


def _rotate(x, same, lo, hi):
    half = ROT_DIM // 2
    x = x.astype(F32)
    return x * same + pltpu.roll(x, half, axis=1) * lo + pltpu.roll(x, LANES - half, axis=1) * hi


def _attn_kernel(lam_ref, q_ref, k_ref, v_ref, same_ref, lo_ref, hi_ref, nw_ref, o_ref,
                 krot, vext, s_buf, m_s, acc_s, *, lam_init, tq, strip, heads):
    qi = pl.program_id(2)
    hw = 2 * A_DH
    lv = lam_ref[...]
    lam = (jnp.exp(jnp.sum(lv[0:1] * lv[1:2], -1, keepdims=True))
           - jnp.exp(jnp.sum(lv[2:3] * lv[3:4], -1, keepdims=True)) + lam_init)
    head_lanes = [slice(g * hw, (g + 1) * hw) for g in range(heads)]

    @pl.when(qi == 0)
    def _():
        for g, hl in enumerate(head_lanes):
            krot[:, hl] = _rotate(k_ref[:, hl], same_ref[...], lo_ref[...], hi_ref[...]).astype(BF16)
            vext[g, :, :hw] = v_ref[:, hl]
            vext[g, :, hw:] = jnp.ones((vext.shape[1], hw), BF16)

    q0 = pl.multiple_of(qi * tq, tq)
    tabs = [t[pl.ds(q0, tq), :] for t in (same_ref, lo_ref, hi_ref)]
    lane = lax.broadcasted_iota(jnp.int32, (tq, hw), 1)
    qq = []
    for hl in head_lanes:
        q = (_rotate(q_ref[:, hl], *tabs) * (A_DH ** -0.5 * LOG2_E)).astype(BF16)
        zero = jnp.zeros_like(q)
        qq.append(jnp.concatenate([jnp.where(lane < A_DH, q, zero), jnp.where(lane >= A_DH, q, zero)],
                                  axis=0))
    m_s[...] = jnp.full(m_s.shape, NEG_BIG, F32)
    acc_s[...] = jnp.zeros_like(acc_s)

    strips = [slice(r, r + strip) for r in range(0, 2 * tq, strip)]

    def scores(g, j, rows):
        r0 = pl.multiple_of(j * tq, tq)
        return lax.dot_general(qq[g][rows], krot[pl.ds(r0, tq), head_lanes[g]], NT_DIMS,
                               preferred_element_type=F32)

    def block(j, diagonal):
        r0 = pl.multiple_of(j * tq, tq)
        for rows in strips:
            for g in range(heads):
                if diagonal:
                    nk = (rows.start & (tq - 1)) + strip
                    s = s_buf[g, rows, :nk]
                    row = (lax.broadcasted_iota(jnp.int32, s.shape, 0) + rows.start) & (tq - 1)
                    col = lax.broadcasted_iota(jnp.int32, s.shape, 1)
                    s = jnp.where(col <= row, s, NEG_BIG)
                else:
                    nk = tq
                    s = s_buf[g, rows, :]
                    s_buf[g, rows, :] = scores(g, j + 1, rows)
                m_prev = m_s[g, rows, :]
                m_new = jnp.maximum(m_prev, jnp.max(s, -1, keepdims=True))
                p = jnp.exp2(s - jnp.concatenate([m_new] * (nk // LANES), axis=1))
                a = jnp.exp2(m_prev - m_new)
                acc_s[g, rows, :] = (jnp.concatenate([a, a], axis=1) * acc_s[g, rows, :]
                                     + jnp.dot(p.astype(BF16), vext[g, pl.ds(r0, nk), :],
                                               preferred_element_type=F32))
                m_s[g, rows, :] = m_new

    def off_diagonal(j, carry):
        block(j, False)
        return carry

    for rows in strips:
        for g in range(heads):
            s_buf[g, rows, :] = scores(g, 0, rows)
    lax.fori_loop(0, qi, off_diagonal, 0)
    block(qi, True)

    for g, hl in enumerate(head_lanes):
        acc = acc_s[g]
        o = acc[:, :hw] / acc[:, hw:]
        o = o[:tq] - lam * o[tq:]
        o_ref[:, hl] = (_rms_norm(o, nw_ref[:, hl]) * (1.0 - lam_init)).astype(o_ref.dtype)


def _attention(proj, tables, lam_vecs, norm_w, lam_init, batch, seq, tq=1024, strip=128, heads=2):
    nq = seq // tq
    hw = 2 * A_DH
    gw = heads * hw
    kv = lambda off: pl.BlockSpec((seq, gw), lambda b, h, i: (b, off // gw + h))
    tab = pl.BlockSpec((seq, LANES), lambda b, h, i: (b, 0))
    return pl.pallas_call(
        functools.partial(_attn_kernel, lam_init=lam_init, tq=tq, strip=min(strip, 2 * tq), heads=heads),
        out_shape=jax.ShapeDtypeStruct((batch * seq, A_W), BF16),
        grid=(batch, A_HEADS // heads, nq),
        in_specs=[pl.BlockSpec((4, A_DH), lambda b, h, i: (0, 0)),
                  pl.BlockSpec((tq, gw), lambda b, h, i: (b * nq + i, OFF_AQ // gw + h)),
                  kv(OFF_AK), kv(OFF_AV), tab, tab, tab,
                  pl.BlockSpec((1, gw), lambda b, h, i: (0, h))],
        out_specs=pl.BlockSpec((tq, gw), lambda b, h, i: (b * nq + i, h)),
        scratch_shapes=[pltpu.VMEM((seq, gw), BF16), pltpu.VMEM((heads, seq, 2 * hw), BF16),
                        pltpu.VMEM((heads, 2 * tq, tq), F32), pltpu.VMEM((heads, 2 * tq, LANES), F32),
                        pltpu.VMEM((heads, 2 * tq, 2 * hw), F32)],
        compiler_params=_params("parallel", "parallel", "arbitrary"),
        name="diff_attn",
    )(lam_vecs, proj, proj, proj, *tables, norm_w.reshape(1, -1))


def _hgrn_kernel(lbl_ref, q_ref, f_ref, v_ref, og_ref, nw_ref, *rest, layer, heads, n_cast):
    out_ref = rest[n_cast]
    qs, ks, bs, st_s = rest[2 * n_cast + 1:]
    _cast_slabs(rest[:n_cast], rest[n_cast + 1:2 * n_cast + 1])
    blk = pl.program_id(2)
    tb = q_ref.shape[0]
    nchunk = tb // CHUNK
    mid = CHUNK // 2 - 1
    head_lanes = [slice(g * G_DK, (g + 1) * G_DK) for g in range(heads)]

    @pl.when(blk == 0)
    def _():
        st_s[...] = jnp.zeros_like(st_s)

    logits = lbl_ref[...]
    e = jnp.exp(logits - jnp.max(logits, 0, keepdims=True))
    p = e / jnp.sum(e, 0, keepdims=True)
    lb = jnp.sum(p[0:layer + 1], 0, keepdims=True) - p[0:1]

    f_g = lb + (1.0 - lb) * _sigmoid(f_ref[...].astype(F32))
    bs[...] = _chunk_cumsum(jnp.log(jnp.maximum(f_g, F_FLOOR)), CHUNK)
    ks[...] = 1.0 - f_g
    qv = q_ref[...].astype(F32)
    qs[...] = qv * _gate_sigmoid(qv)

    def finish(r0, hl, o):
        gg = og_ref[pl.ds(r0, CHUNK), hl].astype(F32)
        hn = _rms_norm(o, nw_ref[:, hl]) * (gg * _gate_sigmoid(gg))
        out_ref[pl.ds(r0, CHUNK), hl] = hn.astype(out_ref.dtype)

    def state_terms(bc, qc, kc, vb, st):
        b_last = bc[CHUNK - 1:CHUNK]
        o_inter = lax.dot_general((qc * jnp.exp(bc)).astype(BF16), st.astype(BF16), NT_DIMS,
                                  preferred_element_type=F32)
        k_dec = (kc * jnp.exp(b_last - bc)).astype(BF16)
        st_new = st * jnp.exp(b_last) + lax.dot_general(vb, k_dec, TN_DIMS, preferred_element_type=F32)
        return o_inter, st_new

    spread = jnp.zeros((1, bs.shape[1]), F32)
    for c in range(nchunk):
        b_mid = bs[c * CHUNK + mid:c * CHUNK + mid + 1, :]
        spread = jnp.maximum(spread, jnp.maximum(bs[c * CHUNK:c * CHUNK + 1, :] - b_mid,
                                                 b_mid - bs[(c + 1) * CHUNK - 1:(c + 1) * CHUNK, :]))
    narrow = jnp.max(spread) <= FACTOR_SPREAD

    @pl.when(narrow)
    def _():
        ri = lax.broadcasted_iota(jnp.int32, (CHUNK, CHUNK), 0)
        ci = lax.broadcasted_iota(jnp.int32, (CHUNK, CHUNK), 1)
        causal = ri >= ci
        st = [st_s[g] for g in range(heads)]
        for c in range(nchunk):
            r0 = c * CHUNK
            for g, hl in enumerate(head_lanes):
                bc, qc, kc = bs[r0:r0 + CHUNK, hl], qs[r0:r0 + CHUNK, hl], ks[r0:r0 + CHUNK, hl]
                vb = v_ref[r0:r0 + CHUNK, hl]
                b_mid = bc[mid:mid + 1]
                q_sc = (qc * jnp.exp(bc - b_mid)).astype(BF16)
                k_sc = (kc * jnp.exp(b_mid - bc)).astype(BF16)
                a = lax.dot_general(q_sc, k_sc, NT_DIMS, preferred_element_type=F32)
                a = jnp.where(causal, a, 0.0).astype(BF16)
                o_inter, st[g] = state_terms(bc, qc, kc, vb, st[g])
                finish(r0, hl, o_inter + jnp.dot(a, vb, preferred_element_type=F32))
        for g in range(heads):
            st_s[g] = st[g]

    @pl.when(jnp.logical_not(narrow))
    def _():
        row_c = lax.broadcasted_iota(jnp.int32, (CHUNK, G_DK), 0)
        row_s = lax.broadcasted_iota(jnp.int32, (SUB, G_DK), 0)

        def chunk_step(c, carry):
            r0 = pl.multiple_of(c * CHUNK, CHUNK)
            for g, hl in enumerate(head_lanes):
                bc = bs[pl.ds(r0, CHUNK), hl]
                qc = qs[pl.ds(r0, CHUNK), hl]
                kc = ks[pl.ds(r0, CHUNK), hl]
                vb = v_ref[pl.ds(r0, CHUNK), hl]
                vc = vb.astype(F32)
                o_inter, st_new = state_terms(bc, qc, kc, vb, st_s[g])
                outs = []
                for i in range(CHUNK // SUB):
                    r = slice(i * SUB, (i + 1) * SUB)
                    b_i, q_i, k_i, v_i = bc[r], qc[r], kc[r], vc[r]
                    o_i = o_inter[r]
                    if i > 0:
                        c_i = b_i[0:1]
                        q_sc = (q_i * jnp.exp(b_i - c_i)).astype(BF16)
                        k_sc = jnp.where(row_c < i * SUB, kc * jnp.exp(jnp.minimum(c_i - bc, 0.0)), 0.0)
                        a_off = lax.dot_general(q_sc, k_sc.astype(BF16), NT_DIMS,
                                                preferred_element_type=F32)
                        o_i = o_i + jnp.dot(a_off.astype(BF16), vb, preferred_element_type=F32)
                    for s in range(SUB):
                        dec = jnp.exp(jnp.minimum(b_i - b_i[s:s + 1], 0.0))
                        a_s = jnp.sum(jnp.where(row_s >= s, q_i * (k_i[s:s + 1] * dec), 0.0), -1,
                                      keepdims=True)
                        o_i = o_i + a_s * v_i[s:s + 1]
                    outs.append(o_i)
                st_s[g] = st_new
                finish(r0, hl, jnp.concatenate(outs, axis=0))
            return carry

        lax.fori_loop(0, nchunk, chunk_step, 0)


def _hgrn(proj, lb_logits, norm_w, layer, cast_weights, batch, seq, tb=1024, heads=4):
    nblk = seq // tb
    nh = G_HEADS // heads
    w = heads * G_DK
    col = lambda off: (lambda b, h, s: (b * nblk + s, off // w + h))
    vec = pl.BlockSpec((1, w), lambda b, h, s: (0, h))
    c_in, c_out, c_shapes = _cast_riders(cast_weights, layer, batch * nh * nblk,
                                         lambda b, h, s: (b * nh + h) * nblk + s)
    return pl.pallas_call(
        functools.partial(_hgrn_kernel, layer=layer, heads=heads, n_cast=len(cast_weights)),
        out_shape=[jax.ShapeDtypeStruct((batch * seq, G_WV), BF16)] + c_shapes,
        grid=(batch, nh, nblk),
        in_specs=[pl.BlockSpec((DEPTH, w), lambda b, h, s: (0, h)),
                  pl.BlockSpec((tb, w), col(OFF_GQ)),
                  pl.BlockSpec((tb, w), col(OFF_GF)),
                  pl.BlockSpec((tb, w), col(OFF_GI)),
                  pl.BlockSpec((tb, w), col(OFF_GG)),
                  vec] + c_in,
        out_specs=[pl.BlockSpec((tb, w), lambda b, h, s: (b * nblk + s, h))] + c_out,
        scratch_shapes=[pltpu.VMEM((tb, w), F32), pltpu.VMEM((tb, w), F32), pltpu.VMEM((tb, w), F32),
                        pltpu.VMEM((heads, G_DK, G_DK), F32)],
        compiler_params=_params("parallel", "parallel", "arbitrary"),
        name="hgrn2",
    )(lb_logits, proj, proj, proj, proj, norm_w.reshape(1, -1), *cast_weights)


def _mix_kernel(hm_ref, ha_ref, hg_ref, g0_ref, g1_ref, g2_ref, pm_ref, pa_ref, pg_ref, wo_ref,
                h_ref, lw_ref, lb_ref, o_ref, obf_ref):
    def branch(x_ref, p_ref, g_ref):
        return (_gate_sigmoid(g_ref[...].astype(F32))
                * jnp.dot(x_ref[...], p_ref[...], preferred_element_type=F32))

    y = branch(hm_ref, pm_ref, g0_ref) + branch(ha_ref, pa_ref, g1_ref) + branch(hg_ref, pg_ref, g2_ref)
    mix = jnp.dot(y.astype(BF16), wo_ref[...], preferred_element_type=F32)
    h = _layer_norm(ALPHA * h_ref[...] + mix, lw_ref[...], lb_ref[...])
    o_ref[...] = h
    obf_ref[...] = h.astype(BF16)


def _mix_ln(hm, ha, hg, gate_pre, p_m, p_a, p_g, w_out, h, ln_w, ln_b, tm=256):
    m, d = h.shape
    row = pl.BlockSpec((tm, d), lambda i: (i, 0))
    vec = pl.BlockSpec((1, d), lambda i: (0, 0))
    act = lambda w: pl.BlockSpec((tm, w), lambda i: (i, 0))
    gate = lambda br: pl.BlockSpec((tm, d), lambda i: (i, br))
    wgt = lambda w: pl.BlockSpec(w.shape, lambda i: (0, 0), pipeline_mode=pl.Buffered(1))
    return pl.pallas_call(
        _mix_kernel,
        out_shape=(jax.ShapeDtypeStruct((m, d), F32), jax.ShapeDtypeStruct((m, d), BF16)),
        grid=(m // tm,),
        in_specs=[act(M_W), act(A_W), act(G_WV), gate(0), gate(1), gate(2),
                  wgt(p_m), wgt(p_a), wgt(p_g), wgt(w_out), row, vec, vec],
        out_specs=(row, row),
        compiler_params=_params("parallel"),
        name="mix_ln",
    )(hm, ha, hg, gate_pre, gate_pre, gate_pre, p_m, p_a, p_g, w_out, h,
      ln_w.reshape(1, d), ln_b.reshape(1, d))


def _mlp_kernel(hbf_ref, wu_ref, wd_ref, h_ref, lw_ref, lb_ref, o_ref, obf_ref):
    j = pl.program_id(1)

    @pl.when(j == 0)
    def _():
        o_ref[...] = jnp.zeros_like(o_ref)

    up = jnp.maximum(jnp.dot(hbf_ref[...], wu_ref[...], preferred_element_type=F32), 0.0)
    o_ref[...] += jnp.dot((up * up).astype(BF16), wd_ref[...], preferred_element_type=F32)

    @pl.when(j == pl.num_programs(1) - 1)
    def _():
        h = _layer_norm(ALPHA * h_ref[...] + o_ref[...], lw_ref[...], lb_ref[...])
        o_ref[...] = h
        obf_ref[...] = h.astype(BF16)


def _mlp_ln(hbf, w_up, w_down, h, ln_w, ln_b, tm=512, tf=1024):
    m, d = h.shape
    f = w_up.shape[1]
    row = pl.BlockSpec((tm, d), lambda i, j: (i, 0))
    vec = pl.BlockSpec((1, d), lambda i, j: (0, 0))
    return pl.pallas_call(
        _mlp_kernel,
        out_shape=(jax.ShapeDtypeStruct((m, d), F32), jax.ShapeDtypeStruct((m, d), BF16)),
        grid=(m // tm, f // tf),
        in_specs=[row, pl.BlockSpec((d, tf), lambda i, j: (0, j)),
                  pl.BlockSpec((tf, d), lambda i, j: (j, 0)), row, vec, vec],
        out_specs=(row, row),
        compiler_params=_params("parallel", "arbitrary"),
        name="mlp_ln",
    )(hbf, w_up, w_down, h, ln_w.reshape(1, d), ln_b.reshape(1, d))


def kernel(x, positions, ln0_w, ln0_b, w_in, m_conv_w, m_conv_b, m_gate_b, m_norm_w, a_lambda, a_norm_w,
           g_lb_logits, g_norm_w, p_m, p_a, p_g, w_out, ln1_w, ln1_b, w_up, w_down, ln2_w, ln2_b):
    batch, seq, d = x.shape
    m = batch * seq
    pos = positions.reshape(m, 1)
    inv = jnp.power(jnp.float32(ROPE_THETA), -jnp.arange(0, ROT_DIM, 2, dtype=F32) / ROT_DIM)
    inv = jnp.tile(inv, LANES // inv.shape[0]).reshape(1, LANES)

    w_t = jnp.swapaxes(w_in, 1, 2)
    row_r = PROJ_M_W + 2 * M_HEADS
    gate_b = jnp.pad(m_gate_b, ((0, 0), (0, LANES - 2 * M_HEADS)))

    tables = _rope_tables(pos, inv)
    h, hbf = _ln(x.reshape(m, d), ln0_w, ln0_b)
    for l in range(DEPTH):
        proj_m = _in_proj(hbf, w_t, l, 0, PROJ_M_W, BF16, 2048, 1024, "in_proj_m")
        proj_r = _in_proj(hbf, w_t, l, row_r, OFF_GATE, BF16, 2048, 1024, "in_proj_r")
        proj_g = _in_proj(hbf, w_t, l, row_r + OFF_GATE, N_BRANCH * D_MODEL, BF16, 2048, 1024, "in_proj_g")

        hm, p_m_l, p_a_l, p_g_l, w_out_l = _mlstm(proj_m, hbf, w_t, gate_b[l:l + 1], m_conv_w[l], m_conv_b[l],
                                                  m_norm_w[l], l, (p_m, p_a, p_g, w_out), batch, seq)
        lam_init = 0.8 - 0.6 * math.exp(-0.3 * l)
        ha = _attention(proj_r, tables, a_lambda[l], a_norm_w[l], lam_init, batch, seq)
        hg, w_up_l, w_down_l = _hgrn(proj_r, g_lb_logits, g_norm_w[l], l, (w_up, w_down), batch, seq)

        h, hbf = _mix_ln(hm, ha, hg, proj_g, p_m_l, p_a_l, p_g_l, w_out_l, h, ln1_w[l], ln1_b[l])
        h, hbf = _mlp_ln(hbf, w_up_l, w_down_l, h, ln2_w[l], ln2_b[l])
    return h.reshape(batch, seq, d)
```

```python
import functools
import math

import jax
import jax.numpy as jnp
from jax import lax
from jax.experimental import pallas as pl
from jax.experimental.pallas import tpu as pltpu

F32 = jnp.float32
BF16 = jnp.bfloat16

D_MODEL = 2048
DEPTH = 2
M_HEADS = 4
M_DH = 256
M_W = M_HEADS * M_DH
CONV_K = 4
A_HEADS = 8
A_DH = 64
A_W = A_HEADS * 2 * A_DH
ROT_DIM = A_DH // 4
ROPE_THETA = 500000.0
G_HEADS = 8
G_DK = 128
G_WK = G_HEADS * G_DK
G_WV = G_HEADS * G_DK
CHUNK = 64
M_CHUNK = 256
SUB = 16
FACTOR_SPREAD = 60.0
N_BRANCH = 3
D_FF = 4 * D_MODEL
ALPHA = (2 * DEPTH) ** 0.25
LN_EPS = 1e-5
NORM_EPS = 1e-6
NEG_BIG = -1e30
LOG2_E = math.log2(math.e)
F_FLOOR = 1e-30

OFF_MQ, OFF_MK, OFF_MV, OFF_MO = 0, M_W, 2 * M_W, 3 * M_W
PROJ_M_W = 4 * M_W
OFF_AQ = 0
OFF_AK = OFF_AQ + A_W
OFF_AV = OFF_AK + A_W
OFF_GQ = OFF_AV + A_W
OFF_GF = OFF_GQ + G_WK
OFF_GI = OFF_GF + G_WK
OFF_GG = OFF_GI + G_WV
OFF_GATE = OFF_GG + G_WV
PROJ_R_W = OFF_GATE + N_BRANCH * D_MODEL
LANES = 128
VMEM_LIMIT = 56 * 1024 * 1024

NT_DIMS = (((1,), (1,)), ((), ()))
TN_DIMS = (((0,), (0,)), ((), ()))


def _params(*sem):
    return pltpu.CompilerParams(dimension_semantics=sem, vmem_limit_bytes=VMEM_LIMIT)


def _sigmoid(x):
    return 1.0 / (1.0 + jnp.exp(-x))


def _gate_sigmoid(x):
    return 0.5 * jnp.tanh(0.5 * x) + 0.5


def _layer_norm(x, w, b):
    mu = jnp.mean(x, -1, keepdims=True)
    xc = x - mu
    var = jnp.mean(xc * xc, -1, keepdims=True)
    return xc * lax.rsqrt(var + LN_EPS) * w + b


def _rms_norm(x, w):
    return x * lax.rsqrt(jnp.mean(x * x, -1, keepdims=True) + NORM_EPS) * w


def _chunk_cumsum(x, chunk):
    row = lax.broadcasted_iota(jnp.int32, x.shape, 0) & (chunk - 1)
    sh = 1
    while sh < chunk:
        x = x + jnp.where(row >= sh, pltpu.roll(x, sh, axis=0), 0.0)
        sh *= 2
    return x


def _cast_riders(weights, layer, steps, step_index):
    ins = [pl.BlockSpec((None, w.shape[1] // steps, w.shape[2]), lambda *g: (layer, step_index(*g), 0))
           for w in weights]
    outs = [pl.BlockSpec((w.shape[1] // steps, w.shape[2]), lambda *g: (step_index(*g), 0)) for w in weights]
    shapes = [jax.ShapeDtypeStruct(w.shape[1:], BF16) for w in weights]
    return ins, outs, shapes


def _interleave(*sections):
    sections = list(sections)
    while sections:
        for sec in list(sections):
            try:
                next(sec)
            except StopIteration:
                sections.remove(sec)


def _cast_slabs(srcs, dsts):
    for src, dst in zip(srcs, dsts):
        dst[...] = src[...].astype(dst.dtype)


def _ln_kernel(x_ref, w_ref, b_ref, o_ref, obf_ref):
    y = _layer_norm(x_ref[...], w_ref[...], b_ref[...])
    o_ref[...] = y
    obf_ref[...] = y.astype(BF16)


def _ln(x, w, b, tm=256):
    m, d = x.shape
    return pl.pallas_call(
        _ln_kernel,
        out_shape=(jax.ShapeDtypeStruct((m, d), F32), jax.ShapeDtypeStruct((m, d), BF16)),
        grid=(m // tm,),
        in_specs=[pl.BlockSpec((tm, d), lambda i: (i, 0)),
                  pl.BlockSpec((1, d), lambda i: (0, 0)),
                  pl.BlockSpec((1, d), lambda i: (0, 0))],
        out_specs=(pl.BlockSpec((tm, d), lambda i: (i, 0)),
                   pl.BlockSpec((tm, d), lambda i: (i, 0))),
        compiler_params=_params("parallel"),
        name="ln0",
    )(x, w.reshape(1, d), b.reshape(1, d))


def _proj_kernel(a_ref, w_ref, *rest, shift):
    o_ref = rest[-1]
    w = w_ref[...]
    if shift:
        w = jnp.concatenate([w[shift:], rest[0][...]], axis=0)
    o_ref[...] = lax.dot_general(a_ref[...], w.astype(BF16), NT_DIMS,
                                 preferred_element_type=F32).astype(o_ref.dtype)


def _in_proj(a, w_t, layer, row0, n, out_dtype, tm, tn, name):
    m, k = a.shape
    shift = row0 % tn
    base = row0 - shift
    in_specs = [pl.BlockSpec((tm, k), lambda i, j: (i, 0)),
                pl.BlockSpec((None, tn, k), lambda i, j: (layer, base // tn + j, 0))]
    if shift:
        in_specs.append(pl.BlockSpec((None, shift, k),
                                     lambda i, j: (layer, (base + (j + 1) * tn) // shift, 0)))
    return pl.pallas_call(
        functools.partial(_proj_kernel, shift=shift),
        out_shape=jax.ShapeDtypeStruct((m, n), out_dtype),
        grid=(m // tm, n // tn),
        in_specs=in_specs,
        out_specs=pl.BlockSpec((tm, tn), lambda i, j: (i, j)),
        compiler_params=_params("parallel", "arbitrary"),
        name=name,
    )(*([a] + [w_t] * (len(in_specs) - 1)))


def _mlstm_init(scratch):
    for ref in scratch:
        ref[...] = jnp.zeros_like(ref)


def _mlstm_block(ins, out_ref, scratch):
    q_ref, k_ref, v_ref, og_ref, x_ref, wg_ref, gb_ref, cw_ref, cb_ref, nw_ref = ins
    qtail, ktail, c_s, n_s, m_s = scratch
    tb = q_ref.shape[0]

    ri = lax.broadcasted_iota(jnp.int32, (tb, tb), 0)
    ci = lax.broadcasted_iota(jnp.int32, (tb, tb), 1)
    causal = ri >= ci
    shifts = [jnp.where(ri - ci == d, 1.0, 0.0).astype(BF16) for d in range(1, CONV_K)]
    row8 = lax.broadcasted_iota(jnp.int32, (8, M_W), 0)

    def conv_taps(x_ref, tail):
        x = x_ref[...]
        t8 = tail[...]
        taps = [x.astype(F32)]
        for d, s in zip(range(1, CONV_K), shifts):
            xd = jnp.dot(s, x, preferred_element_type=F32)
            top = xd[0:8] + jnp.where(row8 < d, pltpu.roll(t8, d, axis=0), 0.0)
            taps.append(jnp.concatenate([top, xd[8:]], axis=0))
        tail[...] = taps[0][tb - 8:tb]
        return taps

    def conv_silu(taps, cols, wofs):
        wcols = slice(wofs + cols.start, wofs + cols.stop)
        y = cb_ref[:, wcols]
        for j in range(CONV_K):
            y = y + cw_ref[j:j + 1, wcols] * taps[CONV_K - 1 - j][:, cols]
        return y * _gate_sigmoid(y)

    q_taps = conv_taps(q_ref, qtail)
    k_taps = conv_taps(k_ref, ktail)

    wg = wg_ref[...]
    wg = jnp.concatenate([wg, jnp.zeros((LANES - wg.shape[0], wg.shape[1]), wg.dtype)], axis=0)
    gb = lax.dot_general(x_ref[...], wg.astype(BF16), NT_DIMS, preferred_element_type=F32) + gb_ref[...]
    log_f = jnp.minimum(gb, 0.0) - jnp.log(1.0 + jnp.exp(-jnp.abs(gb)))
    bcum = _chunk_cumsum(log_f, tb)
    lane = lax.broadcasted_iota(jnp.int32, gb.shape, 1)
    pc = jnp.where(lane < M_HEADS, gb, bcum)
    pt = pc.T

    for h in range(M_HEADS):
        cols = slice(h * M_DH, (h + 1) * M_DH)
        qc = conv_silu(q_taps, cols, 0)
        kc = conv_silu(k_taps, cols, M_W) * (M_DH ** -0.5)
        vc = v_ref[:, cols]
        qb = qc.astype(BF16)

        i_col, b_col = pc[:, h:h + 1], pc[:, M_HEADS + h:M_HEADS + h + 1]
        i_row, b_row = pt[h:h + 1, :], pt[M_HEADS + h:M_HEADS + h + 1, :]
        m_prev = m_s[h]
        c_prev = c_s[h]
        n_prev = n_s[h]

        log_d = jnp.where(causal, b_col + (i_row - b_row), NEG_BIG)
        log_inter = b_col + m_prev
        m_t = jnp.maximum(log_inter, jnp.max(log_d, -1, keepdims=True))
        w_inter = jnp.exp(log_inter - m_t)
        s = lax.dot_general(qb, kc.astype(BF16), NT_DIMS, preferred_element_type=F32)
        s = s * jnp.exp(log_d - m_t)
        num = (jnp.dot(s.astype(BF16), vc, preferred_element_type=F32)
               + w_inter * jnp.dot(qb, c_prev.astype(BF16), preferred_element_type=F32))
        den = jnp.sum(s, -1, keepdims=True) + w_inter * jnp.sum(qc * n_prev, -1, keepdims=True)
        hh = num / jnp.maximum(jnp.abs(den), jnp.exp(-m_t))

        g = b_col[tb - 1:tb, :]
        log_w = g - b_col + i_col
        m_new = jnp.maximum(g + m_prev, jnp.max(log_w, 0, keepdims=True))
        decay = jnp.exp(g + m_prev - m_new)
        kw = kc * jnp.exp(log_w - m_new)
        c_s[h] = decay * c_prev + lax.dot_general(kw.astype(BF16), vc, TN_DIMS,
                                                  preferred_element_type=F32)
        n_s[h] = decay * n_prev + jnp.sum(kw, 0, keepdims=True)
        m_s[h] = m_new

        hn = _rms_norm(hh, nw_ref[:, cols]) * _gate_sigmoid(og_ref[:, cols].astype(F32))
        out_ref[:, cols] = hn.astype(out_ref.dtype)
        yield


def _rope_table_kernel(pos_ref, inv_ref, same_ref, lo_ref, hi_ref):
    ang = pos_ref[...].astype(F32) * inv_ref[...]
    lane = lax.broadcasted_iota(jnp.int32, ang.shape, 1) & (A_DH - 1)
    half = ROT_DIM // 2
    cos, sin = jnp.cos(ang), jnp.sin(ang)
    same_ref[...] = jnp.where(lane < ROT_DIM, cos, 1.0)
    lo_ref[...] = jnp.where((lane >= half) & (lane < ROT_DIM), sin, 0.0)
    hi_ref[...] = jnp.where(lane < half, -sin, 0.0)


def _rope_tables(pos, inv, tm=512):
    m = pos.shape[0]
    out = pl.BlockSpec((tm, LANES), lambda i: (i, 0))
    shp = jax.ShapeDtypeStruct((m, LANES), F32)
    return pl.pallas_call(
        _rope_table_kernel,
        out_shape=(shp, shp, shp),
        grid=(m // tm,),
        in_specs=[pl.BlockSpec((tm, 1), lambda i: (i, 0)), pl.BlockSpec((1, LANES), lambda i: (0, 0))],
        out_specs=(out, out, out),
        compiler_params=_params("parallel"),
        name="rope_tables",
    )(pos, inv)


def _rotate(x, same, lo, hi):
    half = ROT_DIM // 2
    x = x.astype(F32)
    return x * same + pltpu.roll(x, half, axis=1) * lo + pltpu.roll(x, LANES - half, axis=1) * hi


def _attn_kernel(lam_ref, q_ref, k_ref, v_ref, same_ref, lo_ref, hi_ref, nw_ref, o_ref,
                 krot, vext, s_buf, m_s, acc_s, *, lam_init, tq, strip, heads):
    qi = pl.program_id(2)
    hw = 2 * A_DH
    lv = lam_ref[...]
    lam = (jnp.exp(jnp.sum(lv[0:1] * lv[1:2], -1, keepdims=True))
           - jnp.exp(jnp.sum(lv[2:3] * lv[3:4], -1, keepdims=True)) + lam_init)
    head_lanes = [slice(g * hw, (g + 1) * hw) for g in range(heads)]

    @pl.when(qi == 0)
    def _():
        for g, hl in enumerate(head_lanes):
            krot[:, hl] = _rotate(k_ref[:, hl], same_ref[...], lo_ref[...], hi_ref[...]).astype(BF16)
            vext[g, :, :hw] = v_ref[:, hl]
            vext[g, :, hw:] = jnp.ones((vext.shape[1], hw), BF16)

    q0 = pl.multiple_of(qi * tq, tq)
    tabs = [t[pl.ds(q0, tq), :] for t in (same_ref, lo_ref, hi_ref)]
    lane = lax.broadcasted_iota(jnp.int32, (tq, hw), 1)
    qq = []
    for hl in head_lanes:
        q = (_rotate(q_ref[:, hl], *tabs) * (A_DH ** -0.5 * LOG2_E)).astype(BF16)
        zero = jnp.zeros_like(q)
        qq.append(jnp.concatenate([jnp.where(lane < A_DH, q, zero), jnp.where(lane >= A_DH, q, zero)],
                                  axis=0))
    m_s[...] = jnp.full(m_s.shape, NEG_BIG, F32)
    acc_s[...] = jnp.zeros_like(acc_s)

    strips = [slice(r, r + strip) for r in range(0, 2 * tq, strip)]

    def scores(g, j, rows):
        r0 = pl.multiple_of(j * tq, tq)
        return lax.dot_general(qq[g][rows], krot[pl.ds(r0, tq), head_lanes[g]], NT_DIMS,
                               preferred_element_type=F32)

    def block(j, diagonal):
        r0 = pl.multiple_of(j * tq, tq)
        for rows in strips:
            for g in range(heads):
                if diagonal:
                    nk = (rows.start & (tq - 1)) + strip
                    s = s_buf[g, rows, :nk]
                    row = (lax.broadcasted_iota(jnp.int32, s.shape, 0) + rows.start) & (tq - 1)
                    col = lax.broadcasted_iota(jnp.int32, s.shape, 1)
                    s = jnp.where(col <= row, s, NEG_BIG)
                else:
                    nk = tq
                    s = s_buf[g, rows, :]
                    s_buf[g, rows, :] = scores(g, j + 1, rows)
                m_prev = m_s[g, rows, :]
                m_new = jnp.maximum(m_prev, jnp.max(s, -1, keepdims=True))
                p = jnp.exp2(s - jnp.concatenate([m_new] * (nk // LANES), axis=1))
                a = jnp.exp2(m_prev - m_new)
                acc_s[g, rows, :] = (jnp.concatenate([a, a], axis=1) * acc_s[g, rows, :]
                                     + jnp.dot(p.astype(BF16), vext[g, pl.ds(r0, nk), :],
                                               preferred_element_type=F32))
                m_s[g, rows, :] = m_new

    def off_diagonal(j, carry):
        block(j, False)
        return carry

    for rows in strips:
        for g in range(heads):
            s_buf[g, rows, :] = scores(g, 0, rows)
    lax.fori_loop(0, qi, off_diagonal, 0)
    block(qi, True)

    for g, hl in enumerate(head_lanes):
        acc = acc_s[g]
        o = acc[:, :hw] / acc[:, hw:]
        o = o[:tq] - lam * o[tq:]
        o_ref[:, hl] = (_rms_norm(o, nw_ref[:, hl]) * (1.0 - lam_init)).astype(o_ref.dtype)


def _attention(proj, tables, lam_vecs, norm_w, lam_init, batch, seq, tq=1024, strip=128, heads=2):
    nq = seq // tq
    hw = 2 * A_DH
    gw = heads * hw
    kv = lambda off: pl.BlockSpec((seq, gw), lambda b, h, i: (b, off // gw + h))
    tab = pl.BlockSpec((seq, LANES), lambda b, h, i: (b, 0))
    return pl.pallas_call(
        functools.partial(_attn_kernel, lam_init=lam_init, tq=tq, strip=min(strip, 2 * tq), heads=heads),
        out_shape=jax.ShapeDtypeStruct((batch * seq, A_W), BF16),
        grid=(batch, A_HEADS // heads, nq),
        in_specs=[pl.BlockSpec((4, A_DH), lambda b, h, i: (0, 0)),
                  pl.BlockSpec((tq, gw), lambda b, h, i: (b * nq + i, OFF_AQ // gw + h)),
                  kv(OFF_AK), kv(OFF_AV), tab, tab, tab,
                  pl.BlockSpec((1, gw), lambda b, h, i: (0, h))],
        out_specs=pl.BlockSpec((tq, gw), lambda b, h, i: (b * nq + i, h)),
        scratch_shapes=[pltpu.VMEM((seq, gw), BF16), pltpu.VMEM((heads, seq, 2 * hw), BF16),
                        pltpu.VMEM((heads, 2 * tq, tq), F32), pltpu.VMEM((heads, 2 * tq, LANES), F32),
                        pltpu.VMEM((heads, 2 * tq, 2 * hw), F32)],
        compiler_params=_params("parallel", "parallel", "arbitrary"),
        name="diff_attn",
    )(lam_vecs, proj, proj, proj, *tables, norm_w.reshape(1, -1))


def _hgrn_finish(ins, out_ref, r0, hl, o):
    og_ref, nw_ref = ins[4], ins[5]
    gg = og_ref[pl.ds(r0, CHUNK), hl].astype(F32)
    hn = _rms_norm(o, nw_ref[:, hl]) * (gg * _gate_sigmoid(gg))
    out_ref[pl.ds(r0, CHUNK), hl] = hn.astype(out_ref.dtype)


def _hgrn_state_terms(bc, qc, kc, vb, st):
    b_last = bc[CHUNK - 1:CHUNK]
    o_inter = lax.dot_general((qc * jnp.exp(bc)).astype(BF16), st.astype(BF16), NT_DIMS,
                              preferred_element_type=F32)
    k_dec = (kc * jnp.exp(b_last - bc)).astype(BF16)
    st_new = st * jnp.exp(b_last) + lax.dot_general(vb, k_dec, TN_DIMS, preferred_element_type=F32)
    return o_inter, st_new


def _hgrn_prepare(ins, scratch, layer):
    lbl_ref, q_ref, f_ref = ins[:3]
    qs, ks, bs, _ = scratch
    nchunk = q_ref.shape[0] // CHUNK
    mid = CHUNK // 2 - 1

    logits = lbl_ref[...]
    e = jnp.exp(logits - jnp.max(logits, 0, keepdims=True))
    p = e / jnp.sum(e, 0, keepdims=True)
    lb = jnp.sum(p[0:layer + 1], 0, keepdims=True) - p[0:1]

    f_g = lb + (1.0 - lb) * _sigmoid(f_ref[...].astype(F32))
    bs[...] = _chunk_cumsum(jnp.log(jnp.maximum(f_g, F_FLOOR)), CHUNK)
    ks[...] = 1.0 - f_g
    qv = q_ref[...].astype(F32)
    qs[...] = qv * _gate_sigmoid(qv)

    spread = jnp.zeros((1, bs.shape[1]), F32)
    for c in range(nchunk):
        b_mid = bs[c * CHUNK + mid:c * CHUNK + mid + 1, :]
        spread = jnp.maximum(spread, jnp.maximum(bs[c * CHUNK:c * CHUNK + 1, :] - b_mid,
                                                 b_mid - bs[(c + 1) * CHUNK - 1:(c + 1) * CHUNK, :]))
    return jnp.max(spread) <= FACTOR_SPREAD


def _hgrn_fast(ins, out_ref, scratch, heads):
    v_ref = ins[3]
    qs, ks, bs, st_s = scratch
    nchunk = v_ref.shape[0] // CHUNK
    mid = CHUNK // 2 - 1
    head_lanes = [slice(g * G_DK, (g + 1) * G_DK) for g in range(heads)]
    ri = lax.broadcasted_iota(jnp.int32, (CHUNK, CHUNK), 0)
    ci = lax.broadcasted_iota(jnp.int32, (CHUNK, CHUNK), 1)
    causal = ri >= ci
    st = [st_s[g] for g in range(heads)]
    for c in range(nchunk):
        r0 = c * CHUNK
        for g, hl in enumerate(head_lanes):
            bc, qc, kc = bs[r0:r0 + CHUNK, hl], qs[r0:r0 + CHUNK, hl], ks[r0:r0 + CHUNK, hl]
            vb = v_ref[r0:r0 + CHUNK, hl]
            b_mid = bc[mid:mid + 1]
            q_sc = (qc * jnp.exp(bc - b_mid)).astype(BF16)
            k_sc = (kc * jnp.exp(b_mid - bc)).astype(BF16)
            a = lax.dot_general(q_sc, k_sc, NT_DIMS, preferred_element_type=F32)
            a = jnp.where(causal, a, 0.0).astype(BF16)
            o_inter, st[g] = _hgrn_state_terms(bc, qc, kc, vb, st[g])
            _hgrn_finish(ins, out_ref, r0, hl, o_inter + jnp.dot(a, vb, preferred_element_type=F32))
        yield
    for g in range(heads):
        st_s[g] = st[g]


def _hgrn_slow(ins, out_ref, scratch, heads):
    v_ref = ins[3]
    qs, ks, bs, st_s = scratch
    nchunk = v_ref.shape[0] // CHUNK
    head_lanes = [slice(g * G_DK, (g + 1) * G_DK) for g in range(heads)]
    row_c = lax.broadcasted_iota(jnp.int32, (CHUNK, G_DK), 0)
    row_s = lax.broadcasted_iota(jnp.int32, (SUB, G_DK), 0)

    def chunk_step(c, carry):
        r0 = pl.multiple_of(c * CHUNK, CHUNK)
        for g, hl in enumerate(head_lanes):
            bc = bs[pl.ds(r0, CHUNK), hl]
            qc = qs[pl.ds(r0, CHUNK), hl]
            kc = ks[pl.ds(r0, CHUNK), hl]
            vb = v_ref[pl.ds(r0, CHUNK), hl]
            vc = vb.astype(F32)
            o_inter, st_new = _hgrn_state_terms(bc, qc, kc, vb, st_s[g])
            outs = []
            for i in range(CHUNK // SUB):
                r = slice(i * SUB, (i + 1) * SUB)
                b_i, q_i, k_i, v_i = bc[r], qc[r], kc[r], vc[r]
                o_i = o_inter[r]
                if i > 0:
                    c_i = b_i[0:1]
                    q_sc = (q_i * jnp.exp(b_i - c_i)).astype(BF16)
                    k_sc = jnp.where(row_c < i * SUB, kc * jnp.exp(jnp.minimum(c_i - bc, 0.0)), 0.0)
                    a_off = lax.dot_general(q_sc, k_sc.astype(BF16), NT_DIMS,
                                            preferred_element_type=F32)
                    o_i = o_i + jnp.dot(a_off.astype(BF16), vb, preferred_element_type=F32)
                for s in range(SUB):
                    dec = jnp.exp(jnp.minimum(b_i - b_i[s:s + 1], 0.0))
                    a_s = jnp.sum(jnp.where(row_s >= s, q_i * (k_i[s:s + 1] * dec), 0.0), -1,
                                  keepdims=True)
                    o_i = o_i + a_s * v_i[s:s + 1]
                outs.append(o_i)
            st_s[g] = st_new
            _hgrn_finish(ins, out_ref, r0, hl, jnp.concatenate(outs, axis=0))
        return carry

    lax.fori_loop(0, nchunk, chunk_step, 0)


def _recurrent_kernel(*refs, layer, n_cast):
    m_ins, g_ins, rest = refs[:10], refs[10:16], refs[16:]
    hm_ref, hg_ref = rest[n_cast], rest[n_cast + 1]
    m_scr, g_scr = rest[2 * n_cast + 2:2 * n_cast + 7], rest[2 * n_cast + 7:]
    _cast_slabs(rest[:n_cast], rest[n_cast + 2:2 * n_cast + 2])

    @pl.when(pl.program_id(1) == 0)
    def _():
        _mlstm_init(m_scr)
        g_scr[3][...] = jnp.zeros_like(g_scr[3])

    narrow = _hgrn_prepare(g_ins, g_scr, layer)

    @pl.when(narrow)
    def _():
        _interleave(_hgrn_fast(g_ins, hg_ref, g_scr, G_HEADS), _mlstm_block(m_ins, hm_ref, m_scr))

    @pl.when(jnp.logical_not(narrow))
    def _():
        _interleave(_mlstm_block(m_ins, hm_ref, m_scr))
        _hgrn_slow(g_ins, hg_ref, g_scr, G_HEADS)


def _recurrent(proj_m, proj_r, x, w_t, gate_b, conv_w, conv_b, m_norm_w, lb_logits, g_norm_w, layer,
               cast_weights, batch, seq, tb=M_CHUNK):
    nblk = seq // tb
    ng = 2 * M_HEADS
    rows = lambda b, s: b * nblk + s
    mcol = lambda off: pl.BlockSpec((tb, M_W), lambda b, s: (rows(b, s), off // M_W))
    gcol = lambda off: pl.BlockSpec((tb, G_WK), lambda b, s: (rows(b, s), off // G_WK))
    full = lambda a: pl.BlockSpec(a.shape, lambda b, s: (0, 0))
    conv_b, m_norm_w, g_norm_w = conv_b.reshape(1, -1), m_norm_w.reshape(1, -1), g_norm_w.reshape(1, -1)
    c_in, c_out, c_shapes = _cast_riders(cast_weights, layer, batch * nblk, rows)
    out = pl.BlockSpec((tb, M_W), lambda b, s: (rows(b, s), 0))
    return pl.pallas_call(
        functools.partial(_recurrent_kernel, layer=layer, n_cast=len(cast_weights)),
        out_shape=[jax.ShapeDtypeStruct((batch * seq, M_W), BF16),
                   jax.ShapeDtypeStruct((batch * seq, G_WV), BF16)] + c_shapes,
        grid=(batch, nblk),
        in_specs=[mcol(OFF_MQ), mcol(OFF_MK), mcol(OFF_MV), mcol(OFF_MO),
                  pl.BlockSpec((tb, x.shape[1]), lambda b, s: (rows(b, s), 0)),
                  pl.BlockSpec((None, ng, w_t.shape[2]), lambda b, s: (layer, PROJ_M_W // ng, 0)),
                  full(gate_b), full(conv_w), full(conv_b), full(m_norm_w),
                  full(lb_logits), gcol(OFF_GQ), gcol(OFF_GF), gcol(OFF_GI), gcol(OFF_GG),
                  full(g_norm_w)] + c_in,
        out_specs=[out, out] + c_out,
        scratch_shapes=[pltpu.VMEM((8, M_W), F32), pltpu.VMEM((8, M_W), F32),
                        pltpu.VMEM((M_HEADS, M_DH, M_DH), F32), pltpu.VMEM((M_HEADS, 1, M_DH), F32),
                        pltpu.VMEM((M_HEADS, 1, 1), F32),
                        pltpu.VMEM((tb, G_WK), F32), pltpu.VMEM((tb, G_WK), F32), pltpu.VMEM((tb, G_WK), F32),
                        pltpu.VMEM((G_HEADS, G_DK, G_DK), F32)],
        compiler_params=_params("parallel", "arbitrary"),
        name="recurrent",
    )(proj_m, proj_m, proj_m, proj_m, x, w_t, gate_b, conv_w, conv_b, m_norm_w,
      lb_logits, proj_r, proj_r, proj_r, proj_r, g_norm_w, *cast_weights)


def _mix_kernel(hm_ref, ha_ref, hg_ref, g0_ref, g1_ref, g2_ref, pm_ref, pa_ref, pg_ref, wo_ref,
                h_ref, lw_ref, lb_ref, o_ref, obf_ref):
    def branch(x_ref, p_ref, g_ref):
        return (_gate_sigmoid(g_ref[...].astype(F32))
                * jnp.dot(x_ref[...], p_ref[...], preferred_element_type=F32))

    y = branch(hm_ref, pm_ref, g0_ref) + branch(ha_ref, pa_ref, g1_ref) + branch(hg_ref, pg_ref, g2_ref)
    mix = jnp.dot(y.astype(BF16), wo_ref[...], preferred_element_type=F32)
    h = _layer_norm(ALPHA * h_ref[...] + mix, lw_ref[...], lb_ref[...])
    o_ref[...] = h
    obf_ref[...] = h.astype(BF16)


def _mix_ln(hm, ha, hg, gate_pre, p_m, p_a, p_g, w_out, h, ln_w, ln_b, tm=256):
    m, d = h.shape
    row = pl.BlockSpec((tm, d), lambda i: (i, 0))
    vec = pl.BlockSpec((1, d), lambda i: (0, 0))
    act = lambda w: pl.BlockSpec((tm, w), lambda i: (i, 0))
    gate = lambda br: pl.BlockSpec((tm, d), lambda i: (i, br))
    wgt = lambda w: pl.BlockSpec(w.shape, lambda i: (0, 0), pipeline_mode=pl.Buffered(1))
    return pl.pallas_call(
        _mix_kernel,
        out_shape=(jax.ShapeDtypeStruct((m, d), F32), jax.ShapeDtypeStruct((m, d), BF16)),
        grid=(m // tm,),
        in_specs=[act(M_W), act(A_W), act(G_WV), gate(0), gate(1), gate(2),
                  wgt(p_m), wgt(p_a), wgt(p_g), wgt(w_out), row, vec, vec],
        out_specs=(row, row),
        compiler_params=_params("parallel"),
        name="mix_ln",
    )(hm, ha, hg, gate_pre, gate_pre, gate_pre, p_m, p_a, p_g, w_out, h,
      ln_w.reshape(1, d), ln_b.reshape(1, d))


def _mlp_kernel(hbf_ref, wu_ref, wd_ref, h_ref, lw_ref, lb_ref, o_ref, obf_ref):
    j = pl.program_id(1)

    @pl.when(j == 0)
    def _():
        o_ref[...] = jnp.zeros_like(o_ref)

    up = jnp.maximum(jnp.dot(hbf_ref[...], wu_ref[...], preferred_element_type=F32), 0.0)
    o_ref[...] += jnp.dot((up * up).astype(BF16), wd_ref[...], preferred_element_type=F32)

    @pl.when(j == pl.num_programs(1) - 1)
    def _():
        h = _layer_norm(ALPHA * h_ref[...] + o_ref[...], lw_ref[...], lb_ref[...])
        o_ref[...] = h
        obf_ref[...] = h.astype(BF16)


def _mlp_ln(hbf, w_up, w_down, h, ln_w, ln_b, tm=512, tf=1024):
    m, d = h.shape
    f = w_up.shape[1]
    row = pl.BlockSpec((tm, d), lambda i, j: (i, 0))
    vec = pl.BlockSpec((1, d), lambda i, j: (0, 0))
    return pl.pallas_call(
        _mlp_kernel,
        out_shape=(jax.ShapeDtypeStruct((m, d), F32), jax.ShapeDtypeStruct((m, d), BF16)),
        grid=(m // tm, f // tf),
        in_specs=[row, pl.BlockSpec((d, tf), lambda i, j: (0, j)),
                  pl.BlockSpec((tf, d), lambda i, j: (j, 0)), row, vec, vec],
        out_specs=(row, row),
        compiler_params=_params("parallel", "arbitrary"),
        name="mlp_ln",
    )(hbf, w_up, w_down, h, ln_w.reshape(1, d), ln_b.reshape(1, d))


def kernel(x, positions, ln0_w, ln0_b, w_in, m_conv_w, m_conv_b, m_gate_b, m_norm_w, a_lambda, a_norm_w,
           g_lb_logits, g_norm_w, p_m, p_a, p_g, w_out, ln1_w, ln1_b, w_up, w_down, ln2_w, ln2_b):
    batch, seq, d = x.shape
    m = batch * seq
    pos = positions.reshape(m, 1)
    inv = jnp.power(jnp.float32(ROPE_THETA), -jnp.arange(0, ROT_DIM, 2, dtype=F32) / ROT_DIM)
    inv = jnp.tile(inv, LANES // inv.shape[0]).reshape(1, LANES)

    w_t = jnp.swapaxes(w_in, 1, 2)
    row_r = PROJ_M_W + 2 * M_HEADS
    gate_b = jnp.pad(m_gate_b, ((0, 0), (0, LANES - 2 * M_HEADS)))

    tables = _rope_tables(pos, inv)
    h, hbf = _ln(x.reshape(m, d), ln0_w, ln0_b)
    for l in range(DEPTH):
        proj_m = _in_proj(hbf, w_t, l, 0, PROJ_M_W, BF16, 2048, 1024, "in_proj_m")
        proj_r = _in_proj(hbf, w_t, l, row_r, OFF_GATE, BF16, 2048, 1024, "in_proj_r")
        proj_g = _in_proj(hbf, w_t, l, row_r + OFF_GATE, N_BRANCH * D_MODEL, BF16, 2048, 1024, "in_proj_g")

        hm, hg, p_m_l, p_a_l, p_g_l, w_out_l, w_up_l, w_down_l = _recurrent(
            proj_m, proj_r, hbf, w_t, gate_b[l:l + 1], m_conv_w[l], m_conv_b[l], m_norm_w[l], g_lb_logits,
            g_norm_w[l], l, (p_m, p_a, p_g, w_out, w_up, w_down), batch, seq)
        lam_init = 0.8 - 0.6 * math.exp(-0.3 * l)
        ha = _attention(proj_r, tables, a_lambda[l], a_norm_w[l], lam_init, batch, seq)

        h, hbf = _mix_ln(hm, ha, hg, proj_g, p_m_l, p_a_l, p_g_l, w_out_l, h, ln1_w[l], ln1_b[l])
        h, hbf = _mlp_ln(hbf, w_up_l, w_down_l, h, ln2_w[l], ln2_b[l])
    return h.reshape(batch, seq, d)
```
